```python
import jax, jax.numpy as jnp
from jax import lax
import numpy as np

D_MODEL = 2048
BATCH = 8
SEQ = 2048
DEPTH = 1
DEC_BATCH = 32
DEC_SEQ = 1
PAST_LEN = 16384
PAGE_SIZE = 128

A_HEADS = 12
A_HD = 64
A_W = A_HEADS * A_HD
BRANCHES = ((128, 1), (512, 4), (2048, 16))
MAX_WIN = max(w for w, _ in BRANCHES)
A_SCALE = A_HD ** -0.5
NUM_BUCKETS = 32
MAX_DISTANCE = MAX_WIN
POOL_WINDOWS = (2, 4, 8, 16)
POOL_G = len(POOL_WINDOWS)
POOL_W = D_MODEL - A_W
POOL_C = POOL_W // POOL_G
POOL_HIST = max(POOL_WINDOWS) - 1
IN_W = 3 * A_W + POOL_W
N_MEM = 256
X_HEADS = 4
X_HD = 128
X_W = X_HEADS * X_HD
X_SCALE = X_HD ** -0.5
N_GROUPS = 4
E_PER_GROUP = 8
N_EXPERTS = N_GROUPS * E_PER_GROUP
TOP_K = 2
D_EXPERT = 512
MOE_BLOCK = 128
EPS = 1e-6

kernel_name = "hymba_dilated_pool_hmoe_step"


def _rms(x, g):
    x32 = x.astype(jnp.float32)
    y = x32 * lax.rsqrt(jnp.mean(x32 * x32, axis=-1, keepdims=True) + EPS) * g.astype(jnp.float32)
    return y.astype(x.dtype)


def _t5_bucket(dist):
    n = np.asarray(dist, dtype=np.int64)
    exact = NUM_BUCKETS // 2
    ratio = np.log(np.maximum(n, 1) / exact) / np.log(MAX_DISTANCE / exact)
    large = np.minimum(exact + (ratio * (NUM_BUCKETS - exact)).astype(np.int64), NUM_BUCKETS - 1)
    return np.where(n < exact, n, large).astype(np.int32)


def _mix_in(x, norm_g, w_in, q_norm_g, k_norm_g):
    B, T, _ = x.shape
    z = _rms(x, norm_g) @ w_in
    q = _rms(z[..., :A_W].reshape(B, T, A_HEADS, A_HD), q_norm_g)
    k = _rms(z[..., A_W:2 * A_W].reshape(B, T, A_HEADS, A_HD), k_norm_g)
    v = z[..., 2 * A_W:3 * A_W].reshape(B, T, A_HEADS, A_HD)
    u = z[..., 3 * A_W:]
    return q, k, v, u


def _branch_prompt(q, k, v, w, d, rel_bias):
    B, Sp, H, E = q.shape
    nk = w // d
    n = Sp // d
    nb = n // nk

    def strided(t):
        return t.reshape(B, n, d, H, E).transpose(0, 2, 1, 3, 4).reshape(B, d, nb, nk, H, E)

    def band(t):
        prev = jnp.pad(t[:, :, :-1], ((0, 0), (0, 0), (1, 0), (0, 0), (0, 0), (0, 0)))
        return jnp.concatenate([prev, t], axis=3)

    qs, ks, vs = strided(q), strided(k), strided(v)
    rel = (np.arange(nk)[:, None] + nk) - np.arange(2 * nk)[None, :]
    ok = np.broadcast_to((rel >= 0) & (rel <= nk), (nb, nk, 2 * nk)).copy()
    ok[0] &= (np.arange(2 * nk)[None, :] >= nk)
    bias = rel_bias.astype(jnp.float32)[_t5_bucket(np.clip(rel, 0, None) * d)].transpose(2, 0, 1)
    s = jnp.einsum('brnqhe,brnkhe->brnhqk', qs, band(ks), preferred_element_type=jnp.float32)
    s = jnp.where(ok[None, None, :, None], s * A_SCALE + bias, -jnp.inf)
    m = jnp.max(s, axis=-1, keepdims=True)
    p = jnp.exp(s - m)
    l = jnp.sum(p, axis=-1, keepdims=True)
    o = jnp.einsum('brnhqk,brnkhe->brnqhe', p, band(vs).astype(jnp.float32)) / jnp.swapaxes(l, 3, 4)
    lse = jnp.swapaxes((m + jnp.log(l))[..., 0], 3, 4)
    o = o.reshape(B, d, n, H, E).transpose(0, 2, 1, 3, 4).reshape(B, Sp, H, E)
    lse = lse.reshape(B, d, n, H).transpose(0, 2, 1, 3).reshape(B, Sp, H)
    return o, lse


def _branch_step(q, kcat, vcat, w, d, wb, rel_bias):
    T = q.shape[1]
    steps = np.arange(w // d + 1)
    idx = wb + np.arange(T)[:, None] - steps[None, :] * d
    valid = idx >= 0
    idx = np.maximum(idx, 0)
    kg, vg = kcat[:, idx], vcat[:, idx]
    bias = rel_bias.astype(jnp.float32)[_t5_bucket(steps * d)].T
    s = jnp.einsum('bthe,btkhe->bthk', q, kg, preferred_element_type=jnp.float32)
    s = jnp.where(valid[None, :, None, :], s * A_SCALE + bias, -jnp.inf)
    m = jnp.max(s, axis=-1, keepdims=True)
    p = jnp.exp(s - m)
    l = jnp.sum(p, axis=-1, keepdims=True)
    o = jnp.einsum('bthk,btkhe->bthe', p, vg.astype(jnp.float32)) / l
    return o, (m + jnp.log(l))[..., 0]


def _merge_dilations(outs, lses):
    wts = jax.nn.softmax(jnp.stack(lses), axis=0)
    return jnp.sum(wts[..., None] * jnp.stack(outs), axis=0)


def _attn_prompt(q, k, v, rel_bias):
    S = q.shape[1]
    span = int(np.lcm.reduce([w for w, _ in BRANCHES]))
    Sp = -(-S // span) * span
    pad = ((0, 0), (0, Sp - S), (0, 0), (0, 0))
    qp, kp, vp = jnp.pad(q, pad), jnp.pad(k, pad), jnp.pad(v, pad)
    outs, lses = [], []
    for w, d in BRANCHES:
        o, l = _branch_prompt(qp, kp, vp, w, d, rel_bias)
        outs.append(o[:, :S])
        lses.append(l[:, :S])
    return _merge_dilations(outs, lses)


def _attn_step(q, k_new, v_new, cache_k, cache_v, rel_bias):
    wb = cache_k.shape[1]
    kcat = jnp.concatenate([cache_k, k_new], axis=1)
    vcat = jnp.concatenate([cache_v, v_new], axis=1)
    outs, lses = [], []
    for w, d in BRANCHES:
        o, l = _branch_step(q, kcat, vcat, w, d, wb, rel_bias)
        outs.append(o)
        lses.append(l)
    return _merge_dilations(outs, lses), kcat[:, -wb:], vcat[:, -wb:]


def _multi_pool(u_ext, n_prev, start_pos):
    B, L, _ = u_ext.shape
    T = L - n_prev
    u32 = u_ext.astype(jnp.float32)
    cs = jnp.concatenate([jnp.zeros((B, 1, POOL_W), jnp.float32), jnp.cumsum(u32, axis=1)], axis=1)
    i = np.arange(n_prev, L)
    means = []
    for g, w in enumerate(POOL_WINDOWS):
        csg = cs[..., g * POOL_C:(g + 1) * POOL_C]
        lo = np.maximum(i + 1 - w, 0)
        cnt = np.minimum(w, start_pos + np.arange(T) + 1).astype(np.float32)
        means.append((csg[:, i + 1] - csg[:, lo]) / cnt[None, :, None])
    return jnp.stack(means, axis=2) - u32[:, n_prev:].reshape(B, T, POOL_G, POOL_C)


def _mix_out(attn, pooled, pool_w, pool_scale, w_out, dtype):
    B, T = attn.shape[:2]
    pb = jnp.einsum('btgc,gcd->btgd', pooled, pool_w.astype(jnp.float32))
    pb = pb * pool_scale.astype(jnp.float32).reshape(POOL_G, POOL_C)
    mix = jnp.concatenate([attn.reshape(B, T, A_W), pb.reshape(B, T, POOL_W)], axis=-1)
    return mix.astype(dtype) @ w_out


def _mem_kv(mem, norm_mem_g, w_ck, w_cv, ck_norm_g):
    B, M, _ = mem.shape
    hm = _rms(mem, norm_mem_g)
    k = _rms((hm @ w_ck).reshape(B, M, X_HEADS, X_HD), ck_norm_g)
    v = (hm @ w_cv).reshape(B, M, X_HEADS, X_HD)
    return k, v


def _cross(x, mem_k, mem_v, norm_g, w_cq, cq_norm_g, w_co):
    B, T, _ = x.shape
    q = _rms((_rms(x, norm_g) @ w_cq).reshape(B, T, X_HEADS, X_HD), cq_norm_g)
    s = jnp.einsum('bthe,bmhe->bhtm', q, mem_k, preferred_element_type=jnp.float32) * X_SCALE
    p = jax.nn.softmax(s, axis=-1)
    o = jnp.einsum('bhtm,bmhe->bthe', p, mem_v.astype(jnp.float32))
    return o.reshape(B, T, X_W).astype(x.dtype) @ w_co


def _route(h, w_rg, b_rg, w_re, b_re):
    N = h.shape[0]
    lg = (h @ w_rg).astype(jnp.float32) + b_rg.astype(jnp.float32)
    g_sel = lax.top_k(lg, 1)[1][:, 0]
    p_grp = jnp.take_along_axis(jax.nn.softmax(lg, axis=-1), g_sel[:, None], axis=1)[:, 0]
    le = ((h @ w_re).astype(jnp.float32) + b_re.astype(jnp.float32)).reshape(N, N_GROUPS, E_PER_GROUP)
    le = jnp.take_along_axis(le, g_sel[:, None, None], axis=1)[:, 0]
    top_v, top_i = lax.top_k(le, TOP_K)
    gates = jax.nn.softmax(top_v, axis=-1) * p_grp[:, None]
    return g_sel[:, None] * E_PER_GROUP + top_i, gates


def _moe(x, norm_g, w_rg, b_rg, w_re, b_re, w_e_gate, w_e_up, w_e_down):
    B, T, D = x.shape
    N = B * T
    h = _rms(x, norm_g).reshape(N, D)
    experts, gates = _route(h, w_rg, b_rg, w_re, b_re)
    A = N * TOP_K
    blk = MOE_BLOCK
    while blk > 8 and blk * N_EXPERTS > 2 * A:
        blk //= 2
    n_blocks = -(-(A + N_EXPERTS * (blk - 1)) // blk)
    total = n_blocks * blk
    flat_e = experts.reshape(-1)
    flat_t = jnp.repeat(jnp.arange(N, dtype=jnp.int32), TOP_K)
    flat_g = gates.reshape(-1)
    order = jnp.argsort(flat_e)
    se, st, sg = flat_e[order], flat_t[order], flat_g[order]
    counts = jnp.zeros((N_EXPERTS,), jnp.int32).at[flat_e].add(1)
    padded = (counts + blk - 1) // blk * blk
    pend = jnp.cumsum(padded)
    pstart = pend - padded
    cstart = jnp.cumsum(counts) - counts
    dest = pstart[se] + (jnp.arange(A, dtype=jnp.int32) - cstart[se])
    buf_t = jnp.zeros((total,), jnp.int32).at[dest].set(st)
    buf_g = jnp.zeros((total,), jnp.float32).at[dest].set(sg)
    blk_e = jnp.minimum(jnp.searchsorted(pend, jnp.arange(n_blocks) * blk, side='right'), N_EXPERTS - 1)
    xb = h[buf_t].reshape(n_blocks, blk, D)

    def expert_block(args):
        xe, e = args
        a = jax.nn.silu((xe @ w_e_gate[e]).astype(jnp.float32))
        b = (xe @ w_e_up[e]).astype(jnp.float32)
        return ((a * b).astype(xe.dtype) @ w_e_down[e]).astype(jnp.float32)

    yb = lax.map(expert_block, (xb, blk_e)).reshape(total, D)
    y = jnp.zeros((N, D), jnp.float32).at[buf_t].add(yb * buf_g[:, None])
    return y.astype(x.dtype).reshape(B, T, D)


def setup_inputs(seed: int = 0) -> dict:
    key = jax.random.key(seed)
    ks = jax.random.split(key, 40)
    f32 = jnp.float32

    def nrm(k, shape, scale=1.0):
        return jax.random.normal(k, shape, f32) * scale

    def gain(k, shape):
        return 1.0 + 0.02 * jax.random.normal(k, shape, f32)

    wb = min(MAX_WIN, PAST_LEN)
    L = DEPTH
    return {
        "x_prompt": nrm(ks[0], (BATCH, SEQ, D_MODEL)),
        "x_sample": nrm(ks[1], (DEC_BATCH, DEC_SEQ, D_MODEL)),
        "cache_win_k": nrm(ks[2], (L, DEC_BATCH, wb, A_HEADS, A_HD)),
        "cache_win_v": nrm(ks[3], (L, DEC_BATCH, wb, A_HEADS, A_HD)),
        "state_pool": nrm(ks[4], (L, DEC_BATCH, POOL_HIST, POOL_W)),
        "cache_mem_k": nrm(ks[5], (L, DEC_BATCH, N_MEM, X_HEADS, X_HD)),
        "cache_mem_v": nrm(ks[6], (L, DEC_BATCH, N_MEM, X_HEADS, X_HD)),
        "mem_prompt": nrm(ks[7], (BATCH, N_MEM, D_MODEL)),
        "rel_bias": nrm(ks[8], (NUM_BUCKETS, A_HEADS), 0.5),
        "norm_mix_g": gain(ks[9], (L, D_MODEL)),
        "w_in": nrm(ks[10], (L, D_MODEL, IN_W), D_MODEL ** -0.5),
        "q_norm_g": gain(ks[11], (L, A_HD)),
        "k_norm_g": gain(ks[12], (L, A_HD)),
        "pool_w": nrm(ks[13], (L, POOL_G, POOL_C, POOL_C), POOL_C ** -0.5),
        "pool_scale": gain(ks[14], (L, POOL_W)),
        "w_out": nrm(ks[15], (L, A_W + POOL_W, D_MODEL), (A_W + POOL_W) ** -0.5),
        "norm_cross_g": gain(ks[16], (L, D_MODEL)),
        "norm_mem_g": gain(ks[17], (L, D_MODEL)),
        "w_cq": nrm(ks[18], (L, D_MODEL, X_W), D_MODEL ** -0.5),
        "w_ck": nrm(ks[19], (L, D_MODEL, X_W), D_MODEL ** -0.5),
        "w_cv": nrm(ks[20], (L, D_MODEL, X_W), D_MODEL ** -0.5),
        "cq_norm_g": gain(ks[21], (L, X_HD)),
        "ck_norm_g": gain(ks[22], (L, X_HD)),
        "w_co": nrm(ks[23], (L, X_W, D_MODEL), X_W ** -0.5),
        "norm_ffn_g": gain(ks[24], (L, D_MODEL)),
        "w_rg": nrm(ks[25], (L, D_MODEL, N_GROUPS), D_MODEL ** -0.5),
        "b_rg": nrm(ks[26], (L, N_GROUPS), 0.01),
        "w_re": nrm(ks[27], (L, D_MODEL, N_EXPERTS), D_MODEL ** -0.5),
        "b_re": nrm(ks[28], (L, N_EXPERTS), 0.01),
        "w_e_gate": nrm(ks[29], (L, N_EXPERTS, D_MODEL, D_EXPERT), D_MODEL ** -0.5),
        "w_e_up": nrm(ks[30], (L, N_EXPERTS, D_MODEL, D_EXPERT), D_MODEL ** -0.5),
        "w_e_down": nrm(ks[31], (L, N_EXPERTS, D_EXPERT, D_MODEL), D_EXPERT ** -0.5),
    }


def reference(x_prompt, x_sample, cache_win_k, cache_win_v, state_pool, cache_mem_k, cache_mem_v,
              mem_prompt, rel_bias, norm_mix_g, w_in, q_norm_g, k_norm_g, pool_w, pool_scale, w_out,
              norm_cross_g, norm_mem_g, w_cq, w_ck, w_cv, cq_norm_g, ck_norm_g, w_co,
              norm_ffn_g, w_rg, b_rg, w_re, b_re, w_e_gate, w_e_up, w_e_down):
    xp, xs = x_prompt, x_sample
    wkp, wvp, wks, wvs, plp, pls, mkp_l, mvp_l = [], [], [], [], [], [], [], []
    for l in range(DEPTH):
        qp, kp, vp, up = _mix_in(xp, norm_mix_g[l], w_in[l], q_norm_g[l], k_norm_g[l])
        qs, ksn, vsn, us = _mix_in(xs, norm_mix_g[l], w_in[l], q_norm_g[l], k_norm_g[l])
        ap = _attn_prompt(qp, kp, vp, rel_bias)
        asmp, new_k, new_v = _attn_step(qs, ksn, vsn, cache_win_k[l], cache_win_v[l], rel_bias)
        wbp = min(MAX_WIN, xp.shape[1])
        wkp.append(kp[:, -wbp:])
        wvp.append(vp[:, -wbp:])
        wks.append(new_k)
        wvs.append(new_v)
        pp = _multi_pool(up, 0, 0)
        u_ext = jnp.concatenate([state_pool[l], us], axis=1)
        ps = _multi_pool(u_ext, state_pool.shape[2], PAST_LEN)
        plp.append(up[:, -POOL_HIST:])
        pls.append(u_ext[:, -POOL_HIST:])
        xp = xp + _mix_out(ap, pp, pool_w[l], pool_scale[l], w_out[l], xp.dtype)
        xs = xs + _mix_out(asmp, ps, pool_w[l], pool_scale[l], w_out[l], xs.dtype)
        mk, mv = _mem_kv(mem_prompt, norm_mem_g[l], w_ck[l], w_cv[l], ck_norm_g[l])
        mkp_l.append(mk)
        mvp_l.append(mv)
        xp = xp + _cross(xp, mk, mv, norm_cross_g[l], w_cq[l], cq_norm_g[l], w_co[l])
        xs = xs + _cross(xs, cache_mem_k[l], cache_mem_v[l], norm_cross_g[l], w_cq[l], cq_norm_g[l], w_co[l])
        xp = xp + _moe(xp, norm_ffn_g[l], w_rg[l], b_rg[l], w_re[l], b_re[l], w_e_gate[l], w_e_up[l], w_e_down[l])
        xs = xs + _moe(xs, norm_ffn_g[l], w_rg[l], b_rg[l], w_re[l], b_re[l], w_e_gate[l], w_e_up[l], w_e_down[l])
    return (xp, xs, jnp.stack(wkp), jnp.stack(wvp), jnp.stack(wks), jnp.stack(wvs),
            jnp.stack(plp), jnp.stack(pls), jnp.stack(mkp_l), jnp.stack(mvp_l))
```

```python
import functools

import numpy as np
import jax
import jax.numpy as jnp
from jax import lax
from jax.experimental import pallas as pl
from jax.experimental.pallas import tpu as pltpu

F32 = jnp.float32
BF16 = jnp.bfloat16

D_MODEL = 2048
A_HEADS = 12
A_HD = 64
A_W = A_HEADS * A_HD
BRANCHES = ((128, 1), (512, 4), (2048, 16))
NK = 128
NUM_BUCKETS = 32
MAX_DISTANCE = 2048
POOL_WINDOWS = (2, 4, 8, 16)
POOL_W = D_MODEL - A_W
POOL_C = POOL_W // len(POOL_WINDOWS)
POOL_HIST = 15
IN_W = 3 * A_W + POOL_W
N_MEM = 256
X_HEADS = 4
X_HD = 128
X_W = X_HEADS * X_HD
X_SCALE = X_HD ** -0.5
A_SCALE = A_HD ** -0.5
N_GROUPS = 4
E_PER_GROUP = 8
N_EXPERTS = N_GROUPS * E_PER_GROUP
TOP_K = 2
D_EXPERT = 512
EPS = 1e-6
NEG = -1e30

LANES = 128
MXU_N = 256
VMEM_LIMIT = 52 * 1024 * 1024

U_OFF, Q_OFF, K_OFF, V_OFF = 0, POOL_W, POOL_W + A_W, POOL_W + 2 * A_W


def _cparams(sem):
    return pltpu.CompilerParams(dimension_semantics=sem, vmem_limit_bytes=VMEM_LIMIT)


def _rms_rows(x, g):
    ms = jnp.mean(x * x, axis=-1, keepdims=True)
    return x * lax.rsqrt(ms + EPS) * g


def _t5_bucket(dist):
    n = np.asarray(dist, dtype=np.int64)
    exact = NUM_BUCKETS // 2
    ratio = np.log(np.maximum(n, 1) / exact) / np.log(MAX_DISTANCE / exact)
    large = np.minimum(exact + (ratio * (NUM_BUCKETS - exact)).astype(np.int64), NUM_BUCKETS - 1)
    return np.where(n < exact, n, large).astype(np.int32)


def _proj_kernel(x_ref, g_ref, w_ref, cg_ref, o_ref, h_scr, *, norm_lo, norm_hi, group):
    j = pl.program_id(1)

    @pl.when(j == 0)
    def _():
        h_scr[...] = _rms_rows(x_ref[...], g_ref[...]).astype(BF16)

    z = jnp.dot(h_scr[...], w_ref[...], preferred_element_type=F32)
    is_norm = jnp.logical_and(j >= norm_lo, j < norm_hi)

    @pl.when(is_norm)
    def _():
        lo_mask = lax.broadcasted_iota(jnp.int32, (1, LANES), 1) < group
        outs = []
        for c in range(z.shape[1] // LANES):
            zc = z[:, c * LANES:(c + 1) * LANES]
            zz = zc * zc
            if group == LANES:
                r = lax.rsqrt(jnp.sum(zz, -1, keepdims=True) * (1.0 / group) + EPS)
            else:
                lo = jnp.sum(jnp.where(lo_mask, zz, 0.0), -1, keepdims=True) * (1.0 / group)
                hi = jnp.sum(jnp.where(lo_mask, 0.0, zz), -1, keepdims=True) * (1.0 / group)
                r = jnp.where(lo_mask, lax.rsqrt(lo + EPS), lax.rsqrt(hi + EPS))
            outs.append(zc * r)
        o_ref[...] = jnp.concatenate(outs, -1) * cg_ref[...]

    @pl.when(jnp.logical_not(is_norm))
    def _():
        o_ref[...] = z


def _proj(x, g, w_bf, colgain, *, tm, norm_lo, norm_hi, group, name):
    n, d = x.shape
    width = w_bf.shape[1]
    kern = functools.partial(_proj_kernel, norm_lo=norm_lo, norm_hi=norm_hi, group=group)
    return pl.pallas_call(
        kern,
        grid=(n // tm, width // MXU_N),
        in_specs=[
            pl.BlockSpec((tm, d), lambda i, j: (i, 0)),
            pl.BlockSpec((1, d), lambda i, j: (0, 0)),
            pl.BlockSpec((d, MXU_N), lambda i, j: (0, j)),
            pl.BlockSpec((1, MXU_N), lambda i, j: (0, j)),
        ],
        out_specs=pl.BlockSpec((tm, MXU_N), lambda i, j: (i, j)),
        out_shape=jax.ShapeDtypeStruct((n, width), F32),
        scratch_shapes=[pltpu.VMEM((tm, d), BF16)],
        compiler_params=_cparams(("parallel", "arbitrary")),
        name=name,
    )(x, g, w_bf, colgain)


def _attn_prompt_kernel(q_ref, k_ref, v_ref, bias_ref, o_ref, m_scr, l_scr, acc_scr, *, seq):
    lo = lax.broadcasted_iota(jnp.int32, (NK, LANES), 1) < A_HD
    m_scr[...] = jnp.full(m_scr.shape, NEG, F32)
    l_scr[...] = jnp.zeros(l_scr.shape, F32)
    acc_scr[...] = jnp.zeros(acc_scr.shape, F32)

    for br, (_, d) in enumerate(BRANCHES):
        nblk = seq // (d * NK)

        def rows(start, d=d):
            return pl.ds(start, NK, stride=d) if d > 1 else pl.ds(start, NK)

        for r in range(d):
            for qb in range(nblk):
                cur = rows(r + d * NK * qb)
                qf = q_ref[cur, :]
                if qb > 0:
                    prev = rows(r + d * NK * (qb - 1))
                    kk = jnp.concatenate([k_ref[prev, :], k_ref[cur, :]], 0)
                    vv = jnp.concatenate([v_ref[prev, :], v_ref[cur, :]], 0)
                    off = 0
                else:
                    kk, vv, off = k_ref[cur, :], v_ref[cur, :], NK
                kk = kk.astype(BF16)
                vv = vv.astype(BF16)
                reps = kk.shape[0] // NK
                alphas, pvs = [], []
                for h in range(2):
                    qh = jnp.where(lo if h == 0 else jnp.logical_not(lo), qf, 0.0).astype(BF16)
                    s = lax.dot_general(qh, kk, (((1,), (1,)), ((), ())), preferred_element_type=F32)
                    s = s + bias_ref[br, h, :, off:2 * NK]
                    m_old = m_scr[h, cur, :]
                    m_new = jnp.maximum(m_old, jnp.max(s, -1, keepdims=True))
                    p = jnp.exp(s - jnp.concatenate([m_new] * reps, -1))
                    alpha = jnp.exp(m_old - m_new)
                    l_scr[h, cur, :] = alpha * l_scr[h, cur, :] + jnp.sum(p, -1, keepdims=True)
                    m_scr[h, cur, :] = m_new
                    pvs.append(jnp.dot(p.astype(BF16), vv, preferred_element_type=F32))
                    alphas.append(alpha)
                alpha = jnp.where(lo, alphas[0], alphas[1])
                pv = jnp.where(lo, pvs[0], pvs[1])
                acc_scr[cur, :] = alpha * acc_scr[cur, :] + pv

    lo_all = lax.broadcasted_iota(jnp.int32, acc_scr.shape, 1) < A_HD
    l = jnp.where(lo_all, l_scr[0], l_scr[1])
    o_ref[...] = (acc_scr[...] / l).astype(o_ref.dtype)


def _attn_prompt(z3, bias_tab):
    b, seq, _ = z3.shape
    nhp = A_W // LANES
    qo, ko, vo = Q_OFF // LANES, K_OFF // LANES, V_OFF // LANES
    kern = functools.partial(_attn_prompt_kernel, seq=seq)
    return pl.pallas_call(
        kern,
        grid=(b, nhp),
        in_specs=[
            pl.BlockSpec((None, seq, LANES), lambda i, hp: (i, 0, qo + hp)),
            pl.BlockSpec((None, seq, LANES), lambda i, hp: (i, 0, ko + hp)),
            pl.BlockSpec((None, seq, LANES), lambda i, hp: (i, 0, vo + hp)),
            pl.BlockSpec((len(BRANCHES), None, 2, NK, 2 * NK), lambda i, hp: (0, hp, 0, 0, 0)),
        ],
        out_specs=pl.BlockSpec((None, seq, LANES), lambda i, hp: (i, 0, hp)),
        out_shape=jax.ShapeDtypeStruct((b, seq, A_W), BF16),
        scratch_shapes=[
            pltpu.VMEM((2, seq, LANES), F32),
            pltpu.VMEM((2, seq, LANES), F32),
            pltpu.VMEM((seq, LANES), F32),
        ],
        compiler_params=_cparams(("parallel", "parallel")),
        name="attn_prompt",
    )(z3, z3, z3, bias_tab)


def _prompt_bias_table(rel_bias):
    qi = np.arange(NK)[:, None]
    kj = np.arange(2 * NK)[None, :]
    rel = qi + NK - kj
    ok = (rel >= 0) & (rel <= NK)
    tabs = []
    for _, d in BRANCHES:
        bkt = _t5_bucket(np.clip(rel, 0, None) * d)
        bias = rel_bias.astype(F32)[bkt]
        bias = jnp.where(ok[:, :, None], bias, NEG)
        tabs.append(jnp.transpose(bias, (2, 0, 1)))
    tab = jnp.stack(tabs)
    return tab.reshape(len(BRANCHES), A_HEADS // 2, 2, NK, 2 * NK)


def _attn_step_kernel(q_ref, kn_ref, vn_ref, k1_ref, k2_ref, k3_ref, v1_ref, v2_ref, v3_ref,
                      bias_ref, b0_ref, o_ref):
    q = q_ref[...]
    s_new = jnp.sum(kn_ref[...] * q, -1, keepdims=True) + b0_ref[...]
    scores = []
    m = s_new
    for br, kr in enumerate((k1_ref, k2_ref, k3_ref)):
        s = jnp.sum(kr[...] * q[None], -1, keepdims=True) + bias_ref[br]
        scores.append(s)
        m = jnp.maximum(m, jnp.max(s, axis=0))
    p_new = float(len(BRANCHES)) * jnp.exp(s_new - m)
    l = p_new
    acc = p_new * vn_ref[...]
    for s, vr in zip(scores, (v1_ref, v2_ref, v3_ref)):
        p = jnp.exp(s - m[None])
        l = l + jnp.sum(p, axis=0)
        acc = acc + jnp.sum(p * vr[...], axis=0)
    o_ref[...] = acc / l


def _attn_step(qs, kn, vn, cache_k, cache_v, bias_s, bias0):
    b, wb = cache_k.shape[:2]
    in_specs = [pl.BlockSpec((None, A_HEADS, A_HD), lambda i: (i, 0, 0))] * 3
    args = [qs, kn, vn]
    for cache in (cache_k, cache_v):
        for w, d in BRANCHES:
            view = cache.reshape(b, wb // d, d, A_HEADS, A_HD)
            blk = (wb - w) // (d * NK)
            assert blk * d * NK == wb - w
            in_specs.append(pl.BlockSpec((None, NK, None, A_HEADS, A_HD),
                                         lambda i, blk=blk: (i, blk, 0, 0, 0)))
            args.append(view)
    in_specs += [
        pl.BlockSpec((len(BRANCHES), NK, A_HEADS, 1), lambda i: (0, 0, 0, 0)),
        pl.BlockSpec((A_HEADS, 1), lambda i: (0, 0)),
    ]
    return pl.pallas_call(
        _attn_step_kernel,
        grid=(b,),
        in_specs=in_specs,
        out_specs=pl.BlockSpec((None, A_HEADS, A_HD), lambda i: (i, 0, 0)),
        out_shape=jax.ShapeDtypeStruct((b, A_HEADS, A_HD), F32),
        compiler_params=_cparams(("parallel",)),
        name="attn_step",
    )(*args, bias_s, bias0)


def _step_bias_table(rel_bias):
    j = NK - np.arange(NK)
    tabs = [rel_bias.astype(F32)[_t5_bucket(j * d)] for _, d in BRANCHES]
    bias0 = rel_bias.astype(F32)[int(_t5_bucket(np.zeros((1,)))[0])]
    return jnp.stack(tabs)[..., None], bias0[:, None]


def _mix_tail(pooled, attn_bf, x, pw_ref, ps_ref, woa_ref, wop_ref):
    pb = jnp.dot(pooled.astype(BF16), pw_ref[...], preferred_element_type=F32) * ps_ref[...]
    y = jnp.dot(attn_bf, woa_ref[...], preferred_element_type=F32)
    y = y + jnp.dot(pb.astype(BF16), wop_ref[...], preferred_element_type=F32)
    return x + y


def _pick_by_group(vals):
    col = lax.broadcasted_iota(jnp.int32, (1, POOL_W), 1)
    out = vals[-1]
    for g in range(len(vals) - 2, -1, -1):
        out = jnp.where(col < (g + 1) * POOL_C, vals[g], out)
    return out


def _mix_prompt_kernel(u_ref, uh_ref, a_ref, x_ref, pw_ref, ps_ref, woa_ref, wop_ref, o_ref, *, tm):
    i = pl.program_id(1)
    u = u_ref[...]
    halo = jnp.where(i > 0, uh_ref[...], 0.0)
    e = jnp.concatenate([halo, u], 0)
    hr = halo.shape[0]
    sums, s, w = [], e, 1
    while w < POOL_WINDOWS[-1]:
        s = s + pltpu.roll(s, w, 0)
        w *= 2
        sums.append(s)
    pos = i * tm + lax.broadcasted_iota(jnp.int32, (tm, 1), 0)
    means = [sw[hr:] / jnp.minimum(wd, pos + 1).astype(F32) for sw, wd in zip(sums, POOL_WINDOWS)]
    pooled = _pick_by_group(means) - u
    o_ref[...] = _mix_tail(pooled, a_ref[...], x_ref[...], pw_ref, ps_ref, woa_ref, wop_ref)


def _mix_weights_specs(nidx):
    zero = (lambda *_: (0, 0))
    del nidx
    return [
        pl.BlockSpec((POOL_W, POOL_W), zero),
        pl.BlockSpec((1, POOL_W), zero),
        pl.BlockSpec((A_W, D_MODEL), zero),
        pl.BlockSpec((POOL_W, D_MODEL), zero),
    ]


def _mix_prompt(z3, attn, x3, pw_bd, ps, woa, wop, *, tm):
    b, seq, _ = x3.shape
    hr = 16
    kern = functools.partial(_mix_prompt_kernel, tm=tm)
    return pl.pallas_call(
        kern,
        grid=(b, seq // tm),
        in_specs=[
            pl.BlockSpec((None, tm, POOL_W), lambda bi, i: (bi, i, 0)),
            pl.BlockSpec((None, hr, POOL_W), lambda bi, i: (bi, jnp.maximum(i * (tm // hr) - 1, 0), 0)),
            pl.BlockSpec((None, tm, A_W), lambda bi, i: (bi, i, 0)),
            pl.BlockSpec((None, tm, D_MODEL), lambda bi, i: (bi, i, 0)),
        ] + _mix_weights_specs(2),
        out_specs=pl.BlockSpec((None, tm, D_MODEL), lambda bi, i: (bi, i, 0)),
        out_shape=jax.ShapeDtypeStruct((b, seq, D_MODEL), F32),
        compiler_params=_cparams(("parallel", "parallel")),
        name="mix_prompt",
    )(z3, z3, attn, x3, pw_bd, ps, woa, wop)


def _mix_step_kernel(sp_ref, u_ref, a_ref, x_ref, pw_ref, ps_ref, woa_ref, wop_ref, o_ref):
    sp = sp_ref[...]
    u = u_ref[...]
    row = lax.broadcasted_iota(jnp.int32, (1, POOL_HIST, 1), 1)
    means = []
    for w in POOL_WINDOWS:
        hist = jnp.sum(jnp.where(row >= POOL_HIST - (w - 1), sp, 0.0), axis=1)
        means.append((hist + u) / float(w))
    pooled = _pick_by_group(means) - u
    o_ref[...] = _mix_tail(pooled, a_ref[...], x_ref[...], pw_ref, ps_ref, woa_ref, wop_ref)


def _mix_step(sp, us, attn, xs, pw_bd, ps, woa, wop):
    b = xs.shape[0]
    full = lambda shape: pl.BlockSpec(shape, lambda i: (0,) * len(shape))
    return pl.pallas_call(
        _mix_step_kernel,
        grid=(1,),
        in_specs=[full(sp.shape), full(us.shape), full(attn.shape), full(xs.shape),
                  full(pw_bd.shape), full(ps.shape), full(woa.shape), full(wop.shape)],
        out_specs=full((b, D_MODEL)),
        out_shape=jax.ShapeDtypeStruct((b, D_MODEL), F32),
        compiler_params=_cparams(("arbitrary",)),
        name="mix_step",
    )(sp, us, attn, xs, pw_bd, ps, woa, wop)


def _cross_kernel(x_ref, g_ref, wq_ref, cqg_ref, mk_ref, mv_ref, wo_ref, gf_ref, wrh_ref, wrl_ref,
                  br_ref, y_ref, h_ref, lg_ref):
    x = x_ref[...]
    hq = _rms_rows(x, g_ref[...]).astype(BF16)
    q = jnp.dot(hq, wq_ref[...], preferred_element_type=F32)
    outs = []
    for h in range(X_HEADS):
        sl = slice(h * X_HD, (h + 1) * X_HD)
        qh = _rms_rows(q[:, sl], cqg_ref[...]).astype(BF16)
        kh = mk_ref[:, sl].astype(BF16)
        vh = mv_ref[:, sl].astype(BF16)
        s = lax.dot_general(qh, kh, (((1,), (1,)), ((), ())), preferred_element_type=F32) * X_SCALE
        p = jnp.exp(s - jnp.max(s, -1, keepdims=True))
        l = jnp.sum(p, -1, keepdims=True)
        outs.append(jnp.dot(p.astype(BF16), vh, preferred_element_type=F32) / l)
    o = jnp.concatenate(outs, -1).astype(BF16)
    y = x + jnp.dot(o, wo_ref[...], preferred_element_type=F32)
    y_ref[...] = y
    hf = _rms_rows(y, gf_ref[...])
    h_hi = hf.astype(BF16)
    h_lo = (hf - h_hi.astype(F32)).astype(BF16)
    h_ref[...] = h_hi
    lg = jnp.dot(h_hi, wrh_ref[...], preferred_element_type=F32)
    lg = lg + jnp.dot(h_lo, wrh_ref[...], preferred_element_type=F32)
    lg = lg + jnp.dot(h_hi, wrl_ref[...], preferred_element_type=F32)
    lg_ref[...] = lg + br_ref[...]


def _cross(x3, mk, mv, g, wq, cqg, wo, gf, wrh, wrl, brt, *, tm):
    b, seq, _ = x3.shape
    zero = lambda bi, i: (0, 0)
    tok = lambda w: pl.BlockSpec((None, tm, w), lambda bi, i: (bi, i, 0))
    mem = pl.BlockSpec((None, N_MEM, X_W), lambda bi, i: (bi, 0, 0))
    return pl.pallas_call(
        _cross_kernel,
        grid=(b, seq // tm),
        in_specs=[
            tok(D_MODEL),
            pl.BlockSpec((1, D_MODEL), zero),
            pl.BlockSpec((D_MODEL, X_W), zero),
            pl.BlockSpec((1, X_HD), zero),
            mem, mem,
            pl.BlockSpec((X_W, D_MODEL), zero),
            pl.BlockSpec((1, D_MODEL), zero),
            pl.BlockSpec((D_MODEL, LANES), zero),
            pl.BlockSpec((D_MODEL, LANES), zero),
            pl.BlockSpec((1, LANES), zero),
        ],
        out_specs=[tok(D_MODEL), tok(D_MODEL), tok(LANES)],
        out_shape=[
            jax.ShapeDtypeStruct((b, seq, D_MODEL), F32),
            jax.ShapeDtypeStruct((b, seq, D_MODEL), BF16),
            jax.ShapeDtypeStruct((b, seq, LANES), F32),
        ],
        compiler_params=_cparams(("parallel", "parallel")),
        name="cross_b%d" % b,
    )(x3, g, wq, cqg, mk, mv, wo, gf, wrh, wrl, brt)


def _expert_kernel(be_ref, x_ref, wg_ref, wu_ref, wd_ref, gate_ref, o_ref, wg_bf, wu_bf, wd_bf):
    i = pl.program_id(0)
    changed = jnp.logical_or(i == 0, be_ref[i] != be_ref[jnp.maximum(i - 1, 0)])

    @pl.when(changed)
    def _():
        wg_bf[...] = wg_ref[...].astype(BF16)
        wu_bf[...] = wu_ref[...].astype(BF16)
        wd_bf[...] = wd_ref[...].astype(BF16)

    x = x_ref[...]
    a = jax.nn.silu(jnp.dot(x, wg_bf[...], preferred_element_type=F32))
    b = jnp.dot(x, wu_bf[...], preferred_element_type=F32)
    y = jnp.dot((a * b).astype(BF16), wd_bf[...], preferred_element_type=F32)
    o_ref[...] = y * gate_ref[...]


def _experts(blk_e, xb, w_gate, w_up, w_down, gates, *, blk):
    total = xb.shape[0]
    n_blocks = total // blk
    wspec = lambda k, n: pl.BlockSpec((None, k, n), lambda i, be: (be[i], 0, 0))
    return pl.pallas_call(
        _expert_kernel,
        grid_spec=pltpu.PrefetchScalarGridSpec(
            num_scalar_prefetch=1,
            grid=(n_blocks,),
            in_specs=[
                pl.BlockSpec((blk, D_MODEL), lambda i, be: (i, 0)),
                wspec(D_MODEL, D_EXPERT), wspec(D_MODEL, D_EXPERT), wspec(D_EXPERT, D_MODEL),
                pl.BlockSpec((blk, 1), lambda i, be: (i, 0)),
            ],
            out_specs=pl.BlockSpec((blk, D_MODEL), lambda i, be: (i, 0)),
            scratch_shapes=[
                pltpu.VMEM((D_MODEL, D_EXPERT), BF16),
                pltpu.VMEM((D_MODEL, D_EXPERT), BF16),
                pltpu.VMEM((D_EXPERT, D_MODEL), BF16),
            ],
        ),
        out_shape=jax.ShapeDtypeStruct((total, D_MODEL), F32),
        compiler_params=_cparams(("arbitrary",)),
        name="experts",
    )(blk_e, xb, w_gate, w_up, w_down, gates)


def _route(lg):
    n = lg.shape[0]
    lgg = lg[:, :N_GROUPS]
    g_sel = lax.top_k(lgg, 1)[1][:, 0]
    p_grp = jnp.take_along_axis(jax.nn.softmax(lgg, axis=-1), g_sel[:, None], axis=1)[:, 0]
    le = lg[:, N_GROUPS:N_GROUPS + N_EXPERTS].reshape(n, N_GROUPS, E_PER_GROUP)
    le = jnp.take_along_axis(le, g_sel[:, None, None], axis=1)[:, 0]
    top_v, top_i = lax.top_k(le, TOP_K)
    gates = jax.nn.softmax(top_v, axis=-1) * p_grp[:, None]
    return g_sel[:, None] * E_PER_GROUP + top_i, gates


def _moe(h_all, lg_all, w_gate, w_up, w_down, *, blk):
    n = h_all.shape[0]
    experts, gates = _route(lg_all)
    a = n * TOP_K
    n_blocks = -(-(a + N_EXPERTS * (blk - 1)) // blk)
    total = n_blocks * blk
    flat_e = experts.reshape(-1)
    flat_t = jnp.repeat(jnp.arange(n, dtype=jnp.int32), TOP_K)
    onehot = (flat_e[:, None] == jnp.arange(N_EXPERTS, dtype=jnp.int32)[None, :]).astype(jnp.int32)
    rank = jnp.take_along_axis(jnp.cumsum(onehot, axis=0) - onehot, flat_e[:, None], axis=1)[:, 0]
    counts = jnp.sum(onehot, axis=0)
    padded = (counts + blk - 1) // blk * blk
    pend = jnp.cumsum(padded)
    pstart = pend - padded
    dest = pstart[flat_e] + rank
    buf_t = jnp.zeros((total,), jnp.int32).at[dest].set(flat_t)
    buf_g = jnp.zeros((total,), F32).at[dest].set(gates.reshape(-1))
    blk_e = jnp.minimum(jnp.searchsorted(pend, jnp.arange(n_blocks, dtype=jnp.int32) * blk, side="right"),
                        N_EXPERTS - 1).astype(jnp.int32)
    xb = h_all[buf_t]
    yb = _experts(blk_e, xb, w_gate, w_up, w_down, buf_g[:, None], blk=blk)
    return jnp.zeros((n, D_MODEL), F32).at[buf_t].add(yb)


def kernel(x_prompt, x_sample, cache_win_k, cache_win_v, state_pool, cache_mem_k, cache_mem_v,
           mem_prompt, rel_bias, norm_mix_g, w_in, q_norm_g, k_norm_g, pool_w, pool_scale, w_out,
           norm_cross_g, norm_mem_g, w_cq, w_ck, w_cv, cq_norm_g, ck_norm_g, w_co,
           norm_ffn_g, w_rg, b_rg, w_re, b_re, w_e_gate, w_e_up, w_e_down):
    assert norm_mix_g.shape[0] == 1, "single-layer step"
    bp, seq, d = x_prompt.shape
    bs = x_sample.shape[0]
    wb = cache_win_k.shape[2]
    np_tok = bp * seq

    w_in_l = w_in[0]
    w_in_bf = jnp.concatenate([w_in_l[:, 3 * A_W:], w_in_l[:, :3 * A_W]], axis=1).astype(BF16)
    ones = jnp.ones((POOL_W,), F32)
    cg_in = jnp.concatenate([ones, jnp.tile(q_norm_g[0], A_HEADS) * A_SCALE,
                             jnp.tile(k_norm_g[0], A_HEADS), jnp.ones((A_W,), F32)])[None, :]
    g_mix = norm_mix_g[0][None, :]
    pw_bd = jnp.zeros((POOL_W, POOL_W), F32)
    for g in range(len(POOL_WINDOWS)):
        sl = slice(g * POOL_C, (g + 1) * POOL_C)
        pw_bd = pw_bd.at[sl, sl].set(pool_w[0, g])
    pw_bd = pw_bd.astype(BF16)
    ps = pool_scale[0][None, :]
    woa = w_out[0, :A_W].astype(BF16)
    wop = w_out[0, A_W:].astype(BF16)
    w_mem_bf = jnp.concatenate([w_ck[0], w_cv[0]], axis=1).astype(BF16)
    cg_mem = jnp.concatenate([jnp.tile(ck_norm_g[0], X_HEADS), jnp.ones((X_W,), F32)])[None, :]
    wq_bf = w_cq[0].astype(BF16)
    wo_bf = w_co[0].astype(BF16)
    wr = jnp.zeros((d, LANES), F32).at[:, :N_GROUPS].set(w_rg[0]).at[:, N_GROUPS:N_GROUPS + N_EXPERTS].set(w_re[0])
    wr_hi = wr.astype(BF16)
    wr_lo = (wr - wr_hi.astype(F32)).astype(BF16)
    br = jnp.zeros((1, LANES), F32).at[0, :N_GROUPS].set(b_rg[0]).at[0, N_GROUPS:N_GROUPS + N_EXPERTS].set(b_re[0])
    bias_tab = _prompt_bias_table(rel_bias)
    bias_s, bias0 = _step_bias_table(rel_bias)

    proj_in = functools.partial(_proj, g=g_mix, w_bf=w_in_bf, colgain=cg_in,
                                norm_lo=Q_OFF // MXU_N, norm_hi=V_OFF // MXU_N, group=A_HD)

    zp = proj_in(x_prompt.reshape(np_tok, d), tm=1024, name="proj_in_prompt")
    z3 = zp.reshape(bp, seq, IN_W)
    attn_p = _attn_prompt(z3, bias_tab)
    xp1 = _mix_prompt(z3, attn_p, x_prompt, pw_bd, ps, woa, wop, tm=256)

    xs2 = x_sample.reshape(bs, d)
    zs = proj_in(xs2, tm=bs, name="proj_in_step")
    qs = zs[:, Q_OFF:K_OFF].reshape(bs, A_HEADS, A_HD)
    ksn = zs[:, K_OFF:V_OFF].reshape(bs, A_HEADS, A_HD)
    vsn = zs[:, V_OFF:].reshape(bs, A_HEADS, A_HD)
    us = zs[:, :POOL_W]
    attn_s = _attn_step(qs, ksn, vsn, cache_win_k[0], cache_win_v[0], bias_s, bias0)
    xs1 = _mix_step(state_pool[0], us, attn_s.reshape(bs, A_W).astype(BF16), xs2, pw_bd, ps, woa, wop)

    zm = _proj(mem_prompt.reshape(bp * N_MEM, d), norm_mem_g[0][None, :], w_mem_bf, cg_mem,
               tm=512, norm_lo=0, norm_hi=X_W // MXU_N, group=X_HD, name="mem_kv")
    mk_p = zm[:, :X_W].reshape(bp, N_MEM, X_W)
    mv_p = zm[:, X_W:].reshape(bp, N_MEM, X_W)

    cross = functools.partial(_cross, g=norm_cross_g[0][None, :], wq=wq_bf, cqg=cq_norm_g[0][None, :],
                              wo=wo_bf, gf=norm_ffn_g[0][None, :], wrh=wr_hi, wrl=wr_lo, brt=br)
    xp2, hp, lgp = cross(xp1, mk_p, mv_p, tm=512)
    xs2b, hs, lgs = cross(xs1.reshape(bs, 1, d), cache_mem_k[0].reshape(bs, N_MEM, X_W),
                          cache_mem_v[0].reshape(bs, N_MEM, X_W), tm=1)

    h_all = jnp.concatenate([hp.reshape(np_tok, d), hs.reshape(bs, d)], axis=0)
    lg_all = jnp.concatenate([lgp.reshape(np_tok, LANES), lgs.reshape(bs, LANES)], axis=0)
    y_moe = _moe(h_all, lg_all, w_e_gate[0], w_e_up[0], w_e_down[0], blk=256)
    y_prompt = xp2 + y_moe[:np_tok].reshape(bp, seq, d)
    y_sample = xs2b + y_moe[np_tok:].reshape(bs, 1, d)

    win_k_p = z3[:, seq - min(MAX_DISTANCE, seq):, K_OFF:V_OFF].reshape(1, bp, -1, A_HEADS, A_HD)
    win_v_p = z3[:, seq - min(MAX_DISTANCE, seq):, V_OFF:].reshape(1, bp, -1, A_HEADS, A_HD)
    win_k_s = jnp.concatenate([cache_win_k[0][:, 1:], ksn[:, None]], axis=1)[None]
    win_v_s = jnp.concatenate([cache_win_v[0][:, 1:], vsn[:, None]], axis=1)[None]
    pool_p = z3[:, seq - POOL_HIST:, :POOL_W][None]
    pool_s = jnp.concatenate([state_pool[0][:, 1:], us[:, None]], axis=1)[None]
    mem_k_p = mk_p.reshape(1, bp, N_MEM, X_HEADS, X_HD)
    mem_v_p = mv_p.reshape(1, bp, N_MEM, X_HEADS, X_HD)
    del wb
    return (y_prompt, y_sample, win_k_p, win_v_p, win_k_s, win_v_s, pool_p, pool_s, mem_k_p, mem_v_p)
```

```python
import functools

import numpy as np
import jax
import jax.numpy as jnp
from jax import lax
from jax.experimental import pallas as pl
from jax.experimental.pallas import tpu as pltpu

F32 = jnp.float32
BF16 = jnp.bfloat16
U32 = jnp.uint32
I32 = jnp.int32

D_MODEL = 2048
A_HEADS = 12
A_HD = 64
A_W = A_HEADS * A_HD
BRANCHES = ((128, 1), (512, 4), (2048, 16))
NK = 128
NUM_BUCKETS = 32
MAX_DISTANCE = 2048
POOL_WINDOWS = (2, 4, 8, 16)
POOL_W = D_MODEL - A_W
POOL_C = POOL_W // len(POOL_WINDOWS)
POOL_HIST = 15
IN_W = 3 * A_W + POOL_W
N_MEM = 256
X_HEADS = 4
X_HD = 128
X_W = X_HEADS * X_HD
X_SCALE = X_HD ** -0.5
A_SCALE = A_HD ** -0.5
N_GROUPS = 4
E_PER_GROUP = 8
N_EXPERTS = N_GROUPS * E_PER_GROUP
TOP_K = 2
D_EXPERT = 512
EPS = 1e-6
NEG = -1e30

LANES = 128
MXU_N = 256
VMEM_LIMIT = 52 * 1024 * 1024
HALF = D_MODEL // 2

U_OFF, Q_OFF, K_OFF, V_OFF = 0, POOL_W, POOL_W + A_W, POOL_W + 2 * A_W

RT_E0, RT_E1, RT_G0, RT_G1, RT_R0, RT_R1 = range(6)


def _cparams(sem):
    return pltpu.CompilerParams(dimension_semantics=sem, vmem_limit_bytes=VMEM_LIMIT)


def _rms_rows(x, g):
    ms = jnp.mean(x * x, axis=-1, keepdims=True)
    return x * lax.rsqrt(ms + EPS) * g


def _t5_bucket(dist):
    n = np.asarray(dist, dtype=np.int64)
    exact = NUM_BUCKETS // 2
    ratio = np.log(np.maximum(n, 1) / exact) / np.log(MAX_DISTANCE / exact)
    large = np.minimum(exact + (ratio * (NUM_BUCKETS - exact)).astype(np.int64), NUM_BUCKETS - 1)
    return np.where(n < exact, n, large).astype(np.int32)


def _bias_lookup(rel_bias, buckets, ok):
    buckets = np.asarray(buckets, np.int32)
    onehot = (jnp.arange(NUM_BUCKETS, dtype=I32)[:, None] == jnp.asarray(buckets.reshape(1, -1))).astype(F32)
    vals = jnp.einsum("bh,bn->hn", rel_bias.astype(F32), onehot, precision=lax.Precision.HIGHEST)
    vals = jnp.where(jnp.asarray(np.asarray(ok).reshape(1, -1)), vals, NEG)
    return vals.reshape((A_HEADS,) + buckets.shape)


def _proj_kernel(x_ref, g_ref, w_ref, cg_ref, o_ref, h_scr, *, norm_lo, norm_hi, group):
    j = pl.program_id(1)

    @pl.when(j == 0)
    def _():
        h_scr[...] = _rms_rows(x_ref[...], g_ref[...]).astype(BF16)

    z = jnp.dot(h_scr[...], w_ref[...], preferred_element_type=F32)
    is_norm = jnp.logical_and(j >= norm_lo, j < norm_hi)

    @pl.when(is_norm)
    def _():
        lo_mask = lax.broadcasted_iota(I32, (1, LANES), 1) < group
        outs = []
        for c in range(z.shape[1] // LANES):
            zc = z[:, c * LANES:(c + 1) * LANES]
            zz = zc * zc
            if group == LANES:
                r = lax.rsqrt(jnp.sum(zz, -1, keepdims=True) * (1.0 / group) + EPS)
            else:
                lo = jnp.sum(jnp.where(lo_mask, zz, 0.0), -1, keepdims=True) * (1.0 / group)
                hi = jnp.sum(jnp.where(lo_mask, 0.0, zz), -1, keepdims=True) * (1.0 / group)
                r = jnp.where(lo_mask, lax.rsqrt(lo + EPS), lax.rsqrt(hi + EPS))
            outs.append(zc * r)
        o_ref[...] = jnp.concatenate(outs, -1) * cg_ref[...]

    @pl.when(jnp.logical_not(is_norm))
    def _():
        o_ref[...] = z


def _proj(x, g, w_bf, colgain, *, tm, norm_lo, norm_hi, group, name):
    n, d = x.shape
    width = w_bf.shape[1]
    kern = functools.partial(_proj_kernel, norm_lo=norm_lo, norm_hi=norm_hi, group=group)
    return pl.pallas_call(
        kern,
        grid=(n // tm, width // MXU_N),
        in_specs=[
            pl.BlockSpec((tm, d), lambda i, j: (i, 0)),
            pl.BlockSpec((1, d), lambda i, j: (0, 0)),
            pl.BlockSpec((d, MXU_N), lambda i, j: (0, j)),
            pl.BlockSpec((1, MXU_N), lambda i, j: (0, j)),
        ],
        out_specs=pl.BlockSpec((tm, MXU_N), lambda i, j: (i, j)),
        out_shape=jax.ShapeDtypeStruct((n, width), F32),
        scratch_shapes=[pltpu.VMEM((tm, d), BF16)],
        compiler_params=_cparams(("parallel", "arbitrary")),
        name=name,
    )(x, g, w_bf, colgain)


def _attn_prompt_kernel(q_ref, k_ref, v_ref, bias_ref, o_ref, m_scr, l_scr, acc_scr, *, seq):
    lo = lax.broadcasted_iota(I32, (NK, LANES), 1) < A_HD
    m_scr[...] = jnp.full(m_scr.shape, NEG, F32)
    l_scr[...] = jnp.zeros(l_scr.shape, F32)
    acc_scr[...] = jnp.zeros(acc_scr.shape, F32)

    for br, (_, d) in enumerate(BRANCHES):
        nblk = seq // (d * NK)

        def rows(start, d=d):
            return pl.ds(start, NK, stride=d) if d > 1 else pl.ds(start, NK)

        for r in range(d):
            for qb in range(nblk):
                cur = rows(r + d * NK * qb)
                qf = q_ref[cur, :]
                if qb > 0:
                    prev = rows(r + d * NK * (qb - 1))
                    kk = jnp.concatenate([k_ref[prev, :], k_ref[cur, :]], 0)
                    vv = jnp.concatenate([v_ref[prev, :], v_ref[cur, :]], 0)
                    off = 0
                else:
                    kk, vv, off = k_ref[cur, :], v_ref[cur, :], NK
                kk = kk.astype(BF16)
                vv = vv.astype(BF16)
                reps = kk.shape[0] // NK
                alphas, pvs = [], []
                for h in range(2):
                    qh = jnp.where(lo if h == 0 else jnp.logical_not(lo), qf, 0.0).astype(BF16)
                    s = lax.dot_general(qh, kk, (((1,), (1,)), ((), ())), preferred_element_type=F32)
                    s = s + bias_ref[br, h, :, off:2 * NK]
                    m_old = m_scr[h, cur, :]
                    m_new = jnp.maximum(m_old, jnp.max(s, -1, keepdims=True))
                    p = jnp.exp(s - jnp.concatenate([m_new] * reps, -1))
                    alpha = jnp.exp(m_old - m_new)
                    l_scr[h, cur, :] = alpha * l_scr[h, cur, :] + jnp.sum(p, -1, keepdims=True)
                    m_scr[h, cur, :] = m_new
                    pvs.append(jnp.dot(p.astype(BF16), vv, preferred_element_type=F32))
                    alphas.append(alpha)
                alpha = jnp.where(lo, alphas[0], alphas[1])
                pv = jnp.where(lo, pvs[0], pvs[1])
                acc_scr[cur, :] = alpha * acc_scr[cur, :] + pv

    lo_all = lax.broadcasted_iota(I32, acc_scr.shape, 1) < A_HD
    l = jnp.where(lo_all, l_scr[0], l_scr[1])
    o_ref[...] = (acc_scr[...] / l).astype(o_ref.dtype)


def _attn_prompt(z3, bias_tab):
    b, seq, _ = z3.shape
    nhp = A_W // LANES
    qo, ko, vo = Q_OFF // LANES, K_OFF // LANES, V_OFF // LANES
    kern = functools.partial(_attn_prompt_kernel, seq=seq)
    return pl.pallas_call(
        kern,
        grid=(b, nhp),
        in_specs=[
            pl.BlockSpec((None, seq, LANES), lambda i, hp: (i, 0, qo + hp)),
            pl.BlockSpec((None, seq, LANES), lambda i, hp: (i, 0, ko + hp)),
            pl.BlockSpec((None, seq, LANES), lambda i, hp: (i, 0, vo + hp)),
            pl.BlockSpec((len(BRANCHES), None, 2, NK, 2 * NK), lambda i, hp: (0, hp, 0, 0, 0)),
        ],
        out_specs=pl.BlockSpec((None, seq, LANES), lambda i, hp: (i, 0, hp)),
        out_shape=jax.ShapeDtypeStruct((b, seq, A_W), BF16),
        scratch_shapes=[
            pltpu.VMEM((2, seq, LANES), F32),
            pltpu.VMEM((2, seq, LANES), F32),
            pltpu.VMEM((seq, LANES), F32),
        ],
        compiler_params=_cparams(("parallel", "parallel")),
        name="attn_prompt",
    )(z3, z3, z3, bias_tab)


def _prompt_bias_table(rel_bias):
    qi = np.arange(NK)[:, None]
    kj = np.arange(2 * NK)[None, :]
    rel = qi + NK - kj
    ok = (rel >= 0) & (rel <= NK)
    bkt = np.stack([_t5_bucket(np.clip(rel, 0, None) * d) for _, d in BRANCHES])
    tab = _bias_lookup(rel_bias, bkt, np.broadcast_to(ok, bkt.shape))
    return jnp.transpose(tab, (1, 0, 2, 3)).reshape(len(BRANCHES), A_HEADS // 2, 2, NK, 2 * NK)


BR_ROWS = 8


def _step_cache_kernel(q_ref, kn_ref, vn_ref, ck_ref, cv_ref, bias_ref, b0_ref, o_ref, ok_ref, ov_ref,
                       *, hb, wb):
    last = lax.broadcasted_iota(I32, (1, wb), 1) == wb - 1
    for h in range(hb):
        q = q_ref[h]
        kt, vt = ck_ref[h], cv_ref[h]
        kn, vn = kn_ref[h], vn_ref[h]
        s = jnp.sum(kt * q, axis=0, keepdims=True)
        sb = s + bias_ref[h]
        s_new = jnp.sum(kn * q, axis=0, keepdims=True) + b0_ref[h]
        m = jnp.max(jnp.max(sb, axis=1, keepdims=True), axis=0, keepdims=True)
        m = jnp.maximum(m, s_new)
        p = jnp.sum(jnp.exp(sb - m), axis=0, keepdims=True)
        p_new = float(len(BRANCHES)) * jnp.exp(s_new - m)
        l = jnp.sum(p, axis=1, keepdims=True) + p_new
        acc = jnp.sum(vt * p, axis=1, keepdims=True) + p_new * vn
        o_ref[h] = acc / l
        ok_ref[h] = jnp.where(last, kn, pltpu.roll(kt, wb - 1, 1))
        ov_ref[h] = jnp.where(last, vn, pltpu.roll(vt, wb - 1, 1))


def _step_cache(qc, knc, vnc, ck_t, cv_t, bias_s, bias0, *, hb):
    b, _, _, wb = ck_t.shape
    col = pl.BlockSpec((None, hb, A_HD, 1), lambda i, j: (i, j, 0, 0))
    cache = pl.BlockSpec((None, hb, A_HD, wb), lambda i, j: (i, j, 0, 0))
    kern = functools.partial(_step_cache_kernel, hb=hb, wb=wb)
    return pl.pallas_call(
        kern,
        grid=(b, A_HEADS // hb),
        in_specs=[col, col, col, cache, cache,
                  pl.BlockSpec((hb, BR_ROWS, wb), lambda i, j: (j, 0, 0)),
                  pl.BlockSpec((hb, 1, 1), lambda i, j: (j, 0, 0))],
        out_specs=[col, cache, cache],
        out_shape=[jax.ShapeDtypeStruct((b, A_HEADS, A_HD, 1), F32),
                   jax.ShapeDtypeStruct(ck_t.shape, F32),
                   jax.ShapeDtypeStruct(cv_t.shape, F32)],
        compiler_params=_cparams(("parallel", "parallel")),
        name="step_cache",
    )(qc, knc, vnc, ck_t, cv_t, bias_s, bias0)


def _step_bias_table(rel_bias, wb):
    dist = wb - np.arange(wb)
    bkt = np.zeros((BR_ROWS, wb), np.int32)
    ok = np.zeros((BR_ROWS, wb), bool)
    for br, (w, d) in enumerate(BRANCHES):
        bkt[br] = _t5_bucket(dist)
        ok[br] = (dist <= w) & (dist % d == 0)
    tab = _bias_lookup(rel_bias, bkt, ok)
    bias0 = _bias_lookup(rel_bias, _t5_bucket(np.zeros((1, 1))), np.ones((1, 1), bool))
    return tab, bias0


def _mix_tail(pooled, attn_bf, x, pw_ref, ps_ref, woa_ref, wop_ref):
    pb = jnp.dot(pooled.astype(BF16), pw_ref[...], preferred_element_type=F32) * ps_ref[...]
    y = jnp.dot(attn_bf, woa_ref[...], preferred_element_type=F32)
    y = y + jnp.dot(pb.astype(BF16), wop_ref[...], preferred_element_type=F32)
    return x + y


def _pick_by_group(vals):
    col = lax.broadcasted_iota(I32, (1, POOL_W), 1)
    out = vals[-1]
    for g in range(len(vals) - 2, -1, -1):
        out = jnp.where(col < (g + 1) * POOL_C, vals[g], out)
    return out


def _mix_prompt_kernel(u_ref, uh_ref, a_ref, x_ref, pw_ref, ps_ref, woa_ref, wop_ref, o_ref, *, tm):
    i = pl.program_id(1)
    u = u_ref[...]
    halo = jnp.where(i > 0, uh_ref[...], 0.0)
    e = jnp.concatenate([halo, u], 0)
    hr = halo.shape[0]
    sums, s, w = [], e, 1
    while w < POOL_WINDOWS[-1]:
        s = s + pltpu.roll(s, w, 0)
        w *= 2
        sums.append(s)
    pos = i * tm + lax.broadcasted_iota(I32, (tm, 1), 0)
    means = [sw[hr:] / jnp.minimum(wd, pos + 1).astype(F32) for sw, wd in zip(sums, POOL_WINDOWS)]
    pooled = _pick_by_group(means) - u
    o_ref[...] = _mix_tail(pooled, a_ref[...], x_ref[...], pw_ref, ps_ref, woa_ref, wop_ref)


def _mix_weight_specs():
    zero = lambda *_: (0, 0)
    return [
        pl.BlockSpec((POOL_W, POOL_W), zero),
        pl.BlockSpec((1, POOL_W), zero),
        pl.BlockSpec((A_W, D_MODEL), zero),
        pl.BlockSpec((POOL_W, D_MODEL), zero),
    ]


def _mix_prompt(z3, attn, x3, pw_bd, ps, woa, wop, *, tm):
    b, seq, _ = x3.shape
    hr = 16
    kern = functools.partial(_mix_prompt_kernel, tm=tm)
    return pl.pallas_call(
        kern,
        grid=(b, seq // tm),
        in_specs=[
            pl.BlockSpec((None, tm, POOL_W), lambda bi, i: (bi, i, 0)),
            pl.BlockSpec((None, hr, POOL_W), lambda bi, i: (bi, jnp.maximum(i * (tm // hr) - 1, 0), 0)),
            pl.BlockSpec((None, tm, A_W), lambda bi, i: (bi, i, 0)),
            pl.BlockSpec((None, tm, D_MODEL), lambda bi, i: (bi, i, 0)),
        ] + _mix_weight_specs(),
        out_specs=pl.BlockSpec((None, tm, D_MODEL), lambda bi, i: (bi, i, 0)),
        out_shape=jax.ShapeDtypeStruct((b, seq, D_MODEL), F32),
        compiler_params=_cparams(("parallel", "parallel")),
        name="mix_prompt",
    )(z3, z3, attn, x3, pw_bd, ps, woa, wop)


def _mix_step_kernel(sp_ref, u_ref, a_ref, x_ref, pw_ref, ps_ref, woa_ref, wop_ref, o_ref):
    u = u_ref[...]
    means, hist, j = [], u, POOL_HIST
    for w in POOL_WINDOWS:
        while j > POOL_HIST - (w - 1):
            j -= 1
            hist = hist + sp_ref[j]
        means.append(hist / float(w))
    pooled = _pick_by_group(means) - u
    o_ref[...] = _mix_tail(pooled, a_ref[...], x_ref[...], pw_ref, ps_ref, woa_ref, wop_ref)


def _mix_step(sp_t, us, attn, xs, pw_bd, ps, woa, wop):
    b = xs.shape[0]
    full = lambda shape: pl.BlockSpec(shape, lambda i: (0,) * len(shape))
    return pl.pallas_call(
        _mix_step_kernel,
        grid=(1,),
        in_specs=[full(sp_t.shape), full(us.shape), full(attn.shape), full(xs.shape)] + _mix_weight_specs(),
        out_specs=full((b, D_MODEL)),
        out_shape=jax.ShapeDtypeStruct((b, D_MODEL), F32),
        compiler_params=_cparams(("arbitrary",)),
        name="mix_step",
    )(sp_t, us, attn, xs, pw_bd, ps, woa, wop)


def _first_argmax(vals, lane):
    mx = jnp.max(vals, -1, keepdims=True)
    idx = jnp.min(jnp.where(vals == mx, lane, LANES), -1, keepdims=True)
    return mx, idx


def _cross_kernel(x_ref, g_ref, wq_ref, cqg_ref, mk_ref, mv_ref, wo_ref, gf_ref, wrh_ref, wrl_ref,
                  br_ref, y_ref, h_ref, rt_ref, hist_ref):
    x = x_ref[...]
    tm = x.shape[0]
    hq = _rms_rows(x, g_ref[...]).astype(BF16)
    q = jnp.dot(hq, wq_ref[...], preferred_element_type=F32)
    outs = []
    for h in range(X_HEADS):
        sl = slice(h * X_HD, (h + 1) * X_HD)
        qh = _rms_rows(q[:, sl], cqg_ref[...]).astype(BF16)
        kh = mk_ref[:, sl].astype(BF16)
        vh = mv_ref[:, sl].astype(BF16)
        s = lax.dot_general(qh, kh, (((1,), (1,)), ((), ())), preferred_element_type=F32) * X_SCALE
        p = jnp.exp(s - jnp.max(s, -1, keepdims=True))
        l = jnp.sum(p, -1, keepdims=True)
        outs.append(jnp.dot(p.astype(BF16), vh, preferred_element_type=F32) / l)
    o = jnp.concatenate(outs, -1).astype(BF16)
    y = x + jnp.dot(o, wo_ref[...], preferred_element_type=F32)
    y_ref[...] = y

    hf = _rms_rows(y, gf_ref[...])
    h_hi = hf.astype(BF16)
    h_hi32 = h_hi.astype(F32)
    h_lo = (hf - h_hi32).astype(BF16)
    bits = lax.bitcast_convert_type(h_hi32, U32)
    h_ref[...] = bits[:, :HALF] | (bits[:, HALF:] >> 16)

    lg = jnp.dot(h_hi, wrh_ref[...], preferred_element_type=F32)
    lg = lg + jnp.dot(h_lo, wrh_ref[...], preferred_element_type=F32)
    lg = lg + jnp.dot(h_hi, wrl_ref[...], preferred_element_type=F32)
    lg = lg + br_ref[...]

    lane = lax.broadcasted_iota(I32, lg.shape, 1)
    lgg = jnp.where(lane < N_GROUPS, lg, NEG)
    mg, g_sel = _first_argmax(lgg, lane)
    p_grp = 1.0 / jnp.sum(jnp.exp(lgg - mg), -1, keepdims=True)
    e_lo = N_GROUPS + E_PER_GROUP * g_sel
    le = jnp.where(jnp.logical_and(lane >= e_lo, lane < e_lo + E_PER_GROUP), lg, NEG)
    v0, i0 = _first_argmax(le, lane)
    v1, i1 = _first_argmax(jnp.where(lane == i0, NEG, le), lane)
    e1w = jnp.exp(v1 - v0)
    g0 = p_grp / (1.0 + e1w)
    g1 = p_grp * e1w / (1.0 + e1w)
    e0 = i0 - N_GROUPS
    e1 = i1 - N_GROUPS

    cnt = jnp.logical_or(lane == e0, lane == e1).astype(F32)
    if tm > 1:
        ri = lax.broadcasted_iota(I32, (tm, tm), 0)
        ci = lax.broadcasted_iota(I32, (tm, tm), 1)
        before = (ci < ri).astype(BF16)
        pre = jnp.dot(before, cnt.astype(BF16), preferred_element_type=F32)
        r0 = jnp.sum(jnp.where(lane == e0, pre, 0.0), -1, keepdims=True)
        r1 = jnp.sum(jnp.where(lane == e1, pre, 0.0), -1, keepdims=True)
    else:
        r0 = r1 = jnp.zeros_like(g0)
    hist_ref[...] = jnp.sum(cnt, axis=0, keepdims=True)

    rec = jnp.zeros(lg.shape, F32)
    for ln, val in ((RT_E0, e0.astype(F32)), (RT_E1, e1.astype(F32)), (RT_G0, g0), (RT_G1, g1),
                    (RT_R0, r0), (RT_R1, r1)):
        rec = jnp.where(lane == ln, val, rec)
    rt_ref[...] = rec


def _cross(x3, mk, mv, g, wq, cqg, wo, gf, wrh, wrl, brt, *, tm):
    b, seq, _ = x3.shape
    nt = seq // tm
    zero = lambda bi, i: (0, 0)
    tok = lambda w: pl.BlockSpec((None, tm, w), lambda bi, i: (bi, i, 0))
    mem = pl.BlockSpec((None, N_MEM, X_W), lambda bi, i: (bi, 0, 0))
    return pl.pallas_call(
        _cross_kernel,
        grid=(b, nt),
        in_specs=[
            tok(D_MODEL),
            pl.BlockSpec((1, D_MODEL), zero),
            pl.BlockSpec((D_MODEL, X_W), zero),
            pl.BlockSpec((1, X_HD), zero),
            mem, mem,
            pl.BlockSpec((X_W, D_MODEL), zero),
            pl.BlockSpec((1, D_MODEL), zero),
            pl.BlockSpec((D_MODEL, LANES), zero),
            pl.BlockSpec((D_MODEL, LANES), zero),
            pl.BlockSpec((1, LANES), zero),
        ],
        out_specs=[tok(D_MODEL), tok(HALF), tok(LANES),
                   pl.BlockSpec((None, None, 1, LANES), lambda bi, i: (bi, i, 0, 0))],
        out_shape=[
            jax.ShapeDtypeStruct((b, seq, D_MODEL), F32),
            jax.ShapeDtypeStruct((b, seq, HALF), U32),
            jax.ShapeDtypeStruct((b, seq, LANES), F32),
            jax.ShapeDtypeStruct((b, nt, 1, LANES), F32),
        ],
        compiler_params=_cparams(("parallel", "parallel")),
        name="cross_b%d" % b,
    )(x3, g, wq, cqg, mk, mv, wo, gf, wrh, wrl, brt)


def _row_copy(src_ref, s, dst_ref, d, sem):
    return pltpu.make_async_copy(src_ref.at[pl.ds(s, 1)], dst_ref.at[pl.ds(d, 1)], sem)


def _dispatch_kernel(dest_ref, h_ref, xs_in_ref, xs_ref, sem, *, tm):
    del xs_in_ref
    base = pl.program_id(0) * tm

    def issue(n, c):
        for k in range(TOP_K):
            _row_copy(h_ref, n, xs_ref, dest_ref[(base + n) * TOP_K + k], sem).start()
        return c

    lax.fori_loop(0, tm, issue, 0)

    def drain(n, c):
        for k in range(TOP_K):
            _row_copy(h_ref, 0, xs_ref, 0, sem).wait()
        return c

    lax.fori_loop(0, tm, drain, 0)


def _dispatch(dest, h, xs, *, tm):
    n = h.shape[0]
    kern = functools.partial(_dispatch_kernel, tm=tm)
    return pl.pallas_call(
        kern,
        grid_spec=pltpu.PrefetchScalarGridSpec(
            num_scalar_prefetch=1,
            grid=(n // tm,),
            in_specs=[pl.BlockSpec((tm, HALF), lambda i, dest: (i, 0)),
                      pl.BlockSpec(memory_space=pl.ANY)],
            out_specs=pl.BlockSpec(memory_space=pl.ANY),
            scratch_shapes=[pltpu.SemaphoreType.DMA(())],
        ),
        out_shape=jax.ShapeDtypeStruct(xs.shape, xs.dtype),
        input_output_aliases={2: 0},
        compiler_params=_cparams(("arbitrary",)),
        name="dispatch_n%d" % n,
    )(dest, h, xs)


def _expert_kernel(be_ref, nb_ref, x_ref, wg_ref, wu_ref, wd_ref, o_ref, wg_bf, wu_bf, wd_bf):
    i = pl.program_id(0)

    @pl.when(i < nb_ref[0])
    def _():
        changed = jnp.logical_or(i == 0, be_ref[i] != be_ref[jnp.maximum(i - 1, 0)])

        @pl.when(changed)
        def _():
            wg_bf[...] = wg_ref[...].astype(BF16)
            wu_bf[...] = wu_ref[...].astype(BF16)
            wd_bf[...] = wd_ref[...].astype(BF16)

        xu = x_ref[...]
        xa = lax.bitcast_convert_type(xu & jnp.uint32(0xFFFF0000), F32).astype(BF16)
        xb = lax.bitcast_convert_type(xu << 16, F32).astype(BF16)

        def xw(w_bf):
            return (jnp.dot(xa, w_bf[:HALF, :], preferred_element_type=F32)
                    + jnp.dot(xb, w_bf[HALF:, :], preferred_element_type=F32))

        a = jax.nn.silu(xw(wg_bf))
        b = xw(wu_bf)
        o_ref[...] = jnp.dot((a * b).astype(BF16), wd_bf[...], preferred_element_type=F32)


def _experts(blk_e, n_used, xs, w_gate, w_up, w_down, *, blk):
    total = xs.shape[0]
    wspec = lambda k, n: pl.BlockSpec((None, k, n), lambda i, be, nb: (be[i], 0, 0))
    return pl.pallas_call(
        _expert_kernel,
        grid_spec=pltpu.PrefetchScalarGridSpec(
            num_scalar_prefetch=2,
            grid=(total // blk,),
            in_specs=[
                pl.BlockSpec((blk, HALF), lambda i, be, nb: (i, 0)),
                wspec(D_MODEL, D_EXPERT), wspec(D_MODEL, D_EXPERT), wspec(D_EXPERT, D_MODEL),
            ],
            out_specs=pl.BlockSpec((blk, D_MODEL), lambda i, be, nb: (i, 0)),
            scratch_shapes=[
                pltpu.VMEM((D_MODEL, D_EXPERT), BF16),
                pltpu.VMEM((D_MODEL, D_EXPERT), BF16),
                pltpu.VMEM((D_EXPERT, D_MODEL), BF16),
            ],
        ),
        out_shape=jax.ShapeDtypeStruct((total, D_MODEL), F32),
        compiler_params=_cparams(("arbitrary",)),
        name="experts",
    )(blk_e, n_used, xs, w_gate, w_up, w_down)


def _combine_kernel(dest_ref, x_ref, rt_ref, yb_ref, o_ref, buf, sem, *, tm, nt):
    i = pl.program_id(0)

    def issue(t, slot):
        def body(n, c):
            for k in range(TOP_K):
                src = dest_ref[(t * tm + n) * TOP_K + k]
                pltpu.make_async_copy(yb_ref.at[pl.ds(src, 1)], buf.at[slot, k, pl.ds(n, 1)],
                                      sem.at[slot]).start()
            return c
        lax.fori_loop(0, tm, body, 0)

    @pl.when(i == 0)
    def _():
        issue(0, 0)

    @pl.when(i + 1 < nt)
    def _():
        issue(i + 1, (i + 1) % 2)

    slot = i % 2

    def drain(n, c):
        for k in range(TOP_K):
            pltpu.make_async_copy(yb_ref.at[pl.ds(0, 1)], buf.at[slot, k, pl.ds(0, 1)], sem.at[slot]).wait()
        return c

    lax.fori_loop(0, tm, drain, 0)
    rt = rt_ref[...]
    g0 = rt[:, RT_G0:RT_G0 + 1]
    g1 = rt[:, RT_G1:RT_G1 + 1]
    o_ref[...] = x_ref[...] + (g0 * buf[slot, 0] + g1 * buf[slot, 1])


def _combine(dest, x, rt, yb, *, tm):
    n = x.shape[0]
    nt = n // tm
    kern = functools.partial(_combine_kernel, tm=tm, nt=nt)
    return pl.pallas_call(
        kern,
        grid_spec=pltpu.PrefetchScalarGridSpec(
            num_scalar_prefetch=1,
            grid=(nt,),
            in_specs=[pl.BlockSpec((tm, D_MODEL), lambda i, dest: (i, 0)),
                      pl.BlockSpec((tm, LANES), lambda i, dest: (i, 0)),
                      pl.BlockSpec(memory_space=pl.ANY)],
            out_specs=pl.BlockSpec((tm, D_MODEL), lambda i, dest: (i, 0)),
            scratch_shapes=[pltpu.VMEM((2, TOP_K, tm, D_MODEL), F32),
                            pltpu.SemaphoreType.DMA((2,))],
        ),
        out_shape=jax.ShapeDtypeStruct((n, D_MODEL), F32),
        compiler_params=_cparams(("arbitrary",)),
        name="combine_n%d" % n,
    )(dest, x, rt, yb)


def _dispatch_plan(rt_tiles, hist_tiles, tile_sizes, *, blk):
    hist = jnp.concatenate(hist_tiles, axis=0)[:, :N_EXPERTS].astype(I32)
    counts = jnp.sum(hist, axis=0)
    padded = (counts + blk - 1) // blk * blk
    pend = jnp.cumsum(padded)
    pstart = pend - padded
    tile_base = pstart[None, :] + jnp.cumsum(hist, axis=0) - hist
    ids = jnp.arange(N_EXPERTS, dtype=I32)
    dests, t0 = [], 0
    for rt, ht, ts in zip(rt_tiles, hist_tiles, tile_sizes):
        nt = ht.shape[0]
        tb = jnp.repeat(tile_base[t0:t0 + nt], ts, axis=0)
        t0 += nt
        e = rt[:, RT_E0:RT_E1 + 1].astype(I32)
        r = rt[:, RT_R0:RT_R1 + 1].astype(I32)
        base = jnp.sum(jnp.where(e[:, :, None] == ids[None, None, :], tb[:, None, :], 0), axis=-1)
        dests.append((base + r).reshape(-1))
    return dests, pend


def kernel(x_prompt, x_sample, cache_win_k, cache_win_v, state_pool, cache_mem_k, cache_mem_v,
           mem_prompt, rel_bias, norm_mix_g, w_in, q_norm_g, k_norm_g, pool_w, pool_scale, w_out,
           norm_cross_g, norm_mem_g, w_cq, w_ck, w_cv, cq_norm_g, ck_norm_g, w_co,
           norm_ffn_g, w_rg, b_rg, w_re, b_re, w_e_gate, w_e_up, w_e_down):
    assert norm_mix_g.shape[0] == 1, "single-layer step"
    bp, seq, d = x_prompt.shape
    bs = x_sample.shape[0]
    wb = cache_win_k.shape[2]
    np_tok = bp * seq

    w_in_l = w_in[0]
    w_in_bf = jnp.concatenate([w_in_l[:, 3 * A_W:], w_in_l[:, :3 * A_W]], axis=1).astype(BF16)
    cg_in = jnp.concatenate([jnp.ones((POOL_W,), F32), jnp.tile(q_norm_g[0], A_HEADS) * A_SCALE,
                             jnp.tile(k_norm_g[0], A_HEADS), jnp.ones((A_W,), F32)])[None, :]
    g_mix = norm_mix_g[0][None, :]
    ng = len(POOL_WINDOWS)
    pw_bd = jnp.concatenate(
        [jnp.pad(pool_w[0, g], ((0, 0), (g * POOL_C, (ng - 1 - g) * POOL_C))) for g in range(ng)],
        axis=0).astype(BF16)
    ps = pool_scale[0][None, :]
    woa = w_out[0, :A_W].astype(BF16)
    wop = w_out[0, A_W:].astype(BF16)
    w_mem_bf = jnp.concatenate([w_ck[0], w_cv[0]], axis=1).astype(BF16)
    cg_mem = jnp.concatenate([jnp.tile(ck_norm_g[0], X_HEADS), jnp.ones((X_W,), F32)])[None, :]
    wq_bf = w_cq[0].astype(BF16)
    wo_bf = w_co[0].astype(BF16)
    wr = jnp.pad(jnp.concatenate([w_rg[0], w_re[0]], axis=1), ((0, 0), (0, LANES - N_GROUPS - N_EXPERTS)))
    wr_hi = wr.astype(BF16)
    wr_lo = (wr - wr_hi.astype(F32)).astype(BF16)
    br = jnp.pad(jnp.concatenate([b_rg[0], b_re[0]]), (0, LANES - N_GROUPS - N_EXPERTS))[None, :]
    bias_tab = _prompt_bias_table(rel_bias)
    bias_s, bias0 = _step_bias_table(rel_bias, wb)

    proj_in = functools.partial(_proj, g=g_mix, w_bf=w_in_bf, colgain=cg_in,
                                norm_lo=Q_OFF // MXU_N, norm_hi=V_OFF // MXU_N, group=A_HD)

    zp = proj_in(x_prompt.reshape(np_tok, d), tm=1024, name="proj_in_prompt")
    z3 = zp.reshape(bp, seq, IN_W)
    attn_p = _attn_prompt(z3, bias_tab)
    xp1 = _mix_prompt(z3, attn_p, x_prompt, pw_bd, ps, woa, wop, tm=256)

    xs2 = x_sample.reshape(bs, d)
    zs = proj_in(xs2, tm=bs, name="proj_in_step")
    col = lambda off: zs[:, off:off + A_W].reshape(bs, A_HEADS, A_HD, 1)
    us = zs[:, :POOL_W]
    ck_t = jnp.transpose(cache_win_k[0], (0, 2, 3, 1))
    cv_t = jnp.transpose(cache_win_v[0], (0, 2, 3, 1))
    attn_s, nk_t, nv_t = _step_cache(col(Q_OFF), col(K_OFF), col(V_OFF), ck_t, cv_t, bias_s, bias0, hb=4)
    sp_t = jnp.transpose(state_pool[0], (1, 0, 2))
    xs1 = _mix_step(sp_t, us, attn_s.reshape(bs, A_W).astype(BF16), xs2, pw_bd, ps, woa, wop)

    zm = _proj(mem_prompt.reshape(bp * N_MEM, d), norm_mem_g[0][None, :], w_mem_bf, cg_mem,
               tm=512, norm_lo=0, norm_hi=X_W // MXU_N, group=X_HD, name="mem_kv")
    mk_p = zm[:, :X_W].reshape(bp, N_MEM, X_W)
    mv_p = zm[:, X_W:].reshape(bp, N_MEM, X_W)

    cross = functools.partial(_cross, g=norm_cross_g[0][None, :], wq=wq_bf, cqg=cq_norm_g[0][None, :],
                              wo=wo_bf, gf=norm_ffn_g[0][None, :], wrh=wr_hi, wrl=wr_lo, brt=br)
    tm_x = 512
    xp2, hp, rtp, histp = cross(xp1, mk_p, mv_p, tm=tm_x)
    xs2b, hs, rts, hists = cross(xs1.reshape(bs, 1, d), cache_mem_k[0].reshape(bs, N_MEM, X_W),
                                 cache_mem_v[0].reshape(bs, N_MEM, X_W), tm=1)

    blk = 256
    n_all = np_tok + bs
    n_blocks = -(-(n_all * TOP_K + N_EXPERTS * (blk - 1)) // blk)
    rtp2, rts2 = rtp.reshape(np_tok, LANES), rts.reshape(bs, LANES)
    (dest_p, dest_s), pend = _dispatch_plan(
        [rtp2, rts2], [histp.reshape(-1, LANES), hists.reshape(-1, LANES)], [tm_x, 1], blk=blk)
    blk_e = jnp.minimum(jnp.searchsorted(pend, jnp.arange(n_blocks, dtype=I32) * blk, side="right"),
                        N_EXPERTS - 1).astype(I32)
    n_used = (pend[-1:] // blk).astype(I32)
    xs_buf = jnp.zeros((n_blocks * blk, HALF), U32)
    xs_buf = _dispatch(dest_p, hp.reshape(np_tok, HALF), xs_buf, tm=512)
    xs_buf = _dispatch(dest_s, hs.reshape(bs, HALF), xs_buf, tm=bs)
    yb = _experts(blk_e, n_used, xs_buf, w_e_gate[0], w_e_up[0], w_e_down[0], blk=blk)
    y_prompt = _combine(dest_p, xp2.reshape(np_tok, d), rtp2, yb, tm=256).reshape(bp, seq, d)
    y_sample = _combine(dest_s, xs2b.reshape(bs, d), rts2, yb, tm=bs).reshape(bs, 1, d)

    win_k_p = z3[:, seq - min(MAX_DISTANCE, seq):, K_OFF:V_OFF].reshape(1, bp, -1, A_HEADS, A_HD)
    win_v_p = z3[:, seq - min(MAX_DISTANCE, seq):, V_OFF:].reshape(1, bp, -1, A_HEADS, A_HD)
    win_k_s = jnp.transpose(nk_t, (0, 3, 1, 2))[None]
    win_v_s = jnp.transpose(nv_t, (0, 3, 1, 2))[None]
    pool_p = z3[:, seq - POOL_HIST:, :POOL_W][None]
    pool_s = jnp.transpose(jnp.concatenate([sp_t[1:], us[None]], axis=0), (1, 0, 2))[None]
    mem_k_p = mk_p.reshape(1, bp, N_MEM, X_HEADS, X_HD)
    mem_v_p = mv_p.reshape(1, bp, N_MEM, X_HEADS, X_HD)
    return (y_prompt, y_sample, win_k_p, win_v_p, win_k_s, win_v_s, pool_p, pool_s, mem_k_p, mem_v_p)
```

```python
import functools

import numpy as np
import jax
import jax.numpy as jnp
from jax import lax
from jax.experimental import pallas as pl
from jax.experimental.pallas import tpu as pltpu

F32 = jnp.float32
BF16 = jnp.bfloat16
U32 = jnp.uint32
I32 = jnp.int32

D_MODEL = 2048
A_HEADS = 12
A_HD = 64
A_W = A_HEADS * A_HD
BRANCHES = ((128, 1), (512, 4), (2048, 16))
NK = 128
NUM_BUCKETS = 32
MAX_DISTANCE = 2048
POOL_WINDOWS = (2, 4, 8, 16)
POOL_W = D_MODEL - A_W
POOL_C = POOL_W // len(POOL_WINDOWS)
POOL_HIST = 15
IN_W = 3 * A_W + POOL_W
N_MEM = 256
X_HEADS = 4
X_HD = 128
X_W = X_HEADS * X_HD
X_SCALE = X_HD ** -0.5
A_SCALE = A_HD ** -0.5
N_GROUPS = 4
E_PER_GROUP = 8
N_EXPERTS = N_GROUPS * E_PER_GROUP
TOP_K = 2
D_EXPERT = 512
EPS = 1e-6
NEG = -1e30

LANES = 128
MXU_N = 256
VMEM_LIMIT = 52 * 1024 * 1024
HALF = D_MODEL // 2

U_OFF, Q_OFF, K_OFF, V_OFF = 0, POOL_W, POOL_W + A_W, POOL_W + 2 * A_W

RT_E0, RT_E1, RT_G0, RT_G1, RT_R0, RT_R1 = range(6)


def _cparams(sem):
    return pltpu.CompilerParams(dimension_semantics=sem, vmem_limit_bytes=VMEM_LIMIT)


def _rms_rows(x, g):
    ms = jnp.mean(x * x, axis=-1, keepdims=True)
    return x * lax.rsqrt(ms + EPS) * g


def _t5_bucket(dist):
    n = np.asarray(dist, dtype=np.int64)
    exact = NUM_BUCKETS // 2
    ratio = np.log(np.maximum(n, 1) / exact) / np.log(MAX_DISTANCE / exact)
    large = np.minimum(exact + (ratio * (NUM_BUCKETS - exact)).astype(np.int64), NUM_BUCKETS - 1)
    return np.where(n < exact, n, large).astype(np.int32)


def _bias_lookup(rel_bias, buckets, ok):
    buckets = np.asarray(buckets, np.int32)
    onehot = (jnp.arange(NUM_BUCKETS, dtype=I32)[:, None] == jnp.asarray(buckets.reshape(1, -1))).astype(F32)
    vals = jnp.einsum("bh,bn->hn", rel_bias.astype(F32), onehot, precision=lax.Precision.HIGHEST)
    vals = jnp.where(jnp.asarray(np.asarray(ok).reshape(1, -1)), vals, NEG)
    return vals.reshape((A_HEADS,) + buckets.shape)


def _proj_kernel(x_ref, g_ref, w_ref, cg_ref, o_ref, h_scr, *, norm_lo, norm_hi, group):
    j = pl.program_id(1)

    @pl.when(j == 0)
    def _():
        h_scr[...] = _rms_rows(x_ref[...], g_ref[...]).astype(BF16)

    z = jnp.dot(h_scr[...], w_ref[...], preferred_element_type=F32)
    is_norm = jnp.logical_and(j >= norm_lo, j < norm_hi)

    @pl.when(is_norm)
    def _():
        lo_mask = lax.broadcasted_iota(I32, (1, LANES), 1) < group
        outs = []
        for c in range(z.shape[1] // LANES):
            zc = z[:, c * LANES:(c + 1) * LANES]
            zz = zc * zc
            if group == LANES:
                r = lax.rsqrt(jnp.sum(zz, -1, keepdims=True) * (1.0 / group) + EPS)
            else:
                lo = jnp.sum(jnp.where(lo_mask, zz, 0.0), -1, keepdims=True) * (1.0 / group)
                hi = jnp.sum(jnp.where(lo_mask, 0.0, zz), -1, keepdims=True) * (1.0 / group)
                r = jnp.where(lo_mask, lax.rsqrt(lo + EPS), lax.rsqrt(hi + EPS))
            outs.append(zc * r)
        o_ref[...] = jnp.concatenate(outs, -1) * cg_ref[...]

    @pl.when(jnp.logical_not(is_norm))
    def _():
        o_ref[...] = z


def _proj(x, g, w_bf, colgain, *, tm, norm_lo, norm_hi, group, name):
    n, d = x.shape
    width = w_bf.shape[1]
    kern = functools.partial(_proj_kernel, norm_lo=norm_lo, norm_hi=norm_hi, group=group)
    return pl.pallas_call(
        kern,
        grid=(n // tm, width // MXU_N),
        in_specs=[
            pl.BlockSpec((tm, d), lambda i, j: (i, 0)),
            pl.BlockSpec((1, d), lambda i, j: (0, 0)),
            pl.BlockSpec((d, MXU_N), lambda i, j: (0, j)),
            pl.BlockSpec((1, MXU_N), lambda i, j: (0, j)),
        ],
        out_specs=pl.BlockSpec((tm, MXU_N), lambda i, j: (i, j)),
        out_shape=jax.ShapeDtypeStruct((n, width), F32),
        scratch_shapes=[pltpu.VMEM((tm, d), BF16)],
        compiler_params=_cparams(("parallel", "arbitrary")),
        name=name,
    )(x, g, w_bf, colgain)


def _attn_prompt_kernel(q_ref, k_ref, v_ref, bias_ref, o_ref, m_scr, l_scr, acc_scr, *, seq):
    lo = lax.broadcasted_iota(I32, (NK, LANES), 1) < A_HD

    for br, (_, d) in enumerate(BRANCHES):
        nblk = seq // (d * NK)

        def rows(start, d=d):
            return pl.ds(start, NK, stride=d) if d > 1 else pl.ds(start, NK)

        for r in range(d):
            for qb in range(nblk):
                cur = rows(r + d * NK * qb)
                dst = pl.ds((r * nblk + qb) * NK, NK)
                qf = q_ref[cur, :]
                if qb > 0:
                    prev = rows(r + d * NK * (qb - 1))
                    kk = jnp.concatenate([k_ref[prev, :], k_ref[cur, :]], 0)
                    vv = jnp.concatenate([v_ref[prev, :], v_ref[cur, :]], 0)
                    off = 0
                else:
                    kk, vv, off = k_ref[cur, :], v_ref[cur, :], NK
                kk = kk.astype(BF16)
                vv = vv.astype(BF16)
                ms, ls, pvs = [], [], []
                for h in range(2):
                    qh = jnp.where(lo if h == 0 else jnp.logical_not(lo), qf, 0.0).astype(BF16)
                    s = lax.dot_general(qh, kk, (((1,), (1,)), ((), ())), preferred_element_type=F32)
                    s = s + bias_ref[br, h, :, off:2 * NK]
                    m = jnp.max(s, -1, keepdims=True)
                    p = jnp.exp(s - m)
                    ms.append(m)
                    ls.append(jnp.sum(p, -1, keepdims=True))
                    pvs.append(jnp.dot(p.astype(BF16), vv, preferred_element_type=F32))
                acc_scr[br, dst, :] = jnp.where(lo, pvs[0], pvs[1])
                m_scr[br, dst, :] = jnp.where(lo, ms[0], ms[1])
                l_scr[br, dst, :] = jnp.where(lo, ls[0], ls[1])

    d4, d16 = BRANCHES[1][1], BRANCHES[2][1]
    for r in range(d16):
        for qb in range(seq // (d16 * NK)):
            nat = pl.ds(r + d16 * NK * qb, NK, stride=d16)
            views = (nat,
                     pl.ds((r % d4) * (seq // d4) + r // d4 + (d16 // d4) * NK * qb, NK, stride=d16 // d4),
                     pl.ds(r * (seq // d16) + NK * qb, NK))
            m = [m_scr[i, v, :] for i, v in enumerate(views)]
            mx = jnp.maximum(jnp.maximum(m[0], m[1]), m[2])
            w = [jnp.exp(mi - mx) for mi in m]
            num = sum(wi * acc_scr[i, v, :] for i, (wi, v) in enumerate(zip(w, views)))
            den = sum(wi * l_scr[i, v, :] for i, (wi, v) in enumerate(zip(w, views)))
            o_ref[nat, :] = num / den


def _attn_prompt(z3, bias_tab):
    b, seq, _ = z3.shape
    assert seq % BRANCHES[-1][0] == 0, "sequence must be a multiple of the widest window"
    nhp = A_W // LANES
    nbr = len(BRANCHES)
    qo, ko, vo = Q_OFF // LANES, K_OFF // LANES, V_OFF // LANES
    kern = functools.partial(_attn_prompt_kernel, seq=seq)
    return pl.pallas_call(
        kern,
        grid=(b, nhp),
        in_specs=[
            pl.BlockSpec((None, seq, LANES), lambda i, hp: (i, 0, qo + hp)),
            pl.BlockSpec((None, seq, LANES), lambda i, hp: (i, 0, ko + hp)),
            pl.BlockSpec((None, seq, LANES), lambda i, hp: (i, 0, vo + hp)),
            pl.BlockSpec((len(BRANCHES), None, 2, NK, 2 * NK), lambda i, hp: (0, hp, 0, 0, 0)),
        ],
        out_specs=pl.BlockSpec((None, seq, LANES), lambda i, hp: (i, 0, hp)),
        out_shape=jax.ShapeDtypeStruct((b, seq, A_W), F32),
        scratch_shapes=[pltpu.VMEM((nbr, seq, LANES), F32)] * 3,
        compiler_params=_cparams(("parallel", "parallel")),
        name="attn_prompt",
    )(z3, z3, z3, bias_tab)


def _prompt_bias_table(rel_bias):
    qi = np.arange(NK)[:, None]
    kj = np.arange(2 * NK)[None, :]
    rel = qi + NK - kj
    ok = (rel >= 0) & (rel <= NK)
    bkt = np.stack([_t5_bucket(np.clip(rel, 0, None) * d) for _, d in BRANCHES])
    tab = _bias_lookup(rel_bias, bkt, np.broadcast_to(ok, bkt.shape))
    return jnp.transpose(tab, (1, 0, 2, 3)).reshape(len(BRANCHES), A_HEADS // 2, 2, NK, 2 * NK)


BR_ROWS = 8


def _step_cache_kernel(q_ref, kn_ref, vn_ref, ck_ref, cv_ref, bias_ref, b0_ref, o_ref, ok_ref, ov_ref,
                       *, hb, wb):
    last = lax.broadcasted_iota(I32, (1, wb), 1) == wb - 1
    rnd = lambda t: t.astype(BF16).astype(F32)
    for h in range(hb):
        q = rnd(q_ref[h])
        kt, vt = ck_ref[h], cv_ref[h]
        kn, vn = kn_ref[h], vn_ref[h]
        s = jnp.sum(rnd(kt) * q, axis=0, keepdims=True)
        sb = s + bias_ref[h]
        s_new = jnp.sum(rnd(kn) * q, axis=0, keepdims=True) + b0_ref[h]
        m = jnp.maximum(jnp.max(sb, axis=1, keepdims=True), s_new)
        p = jnp.exp(sb - m)
        p_new = jnp.exp(s_new - m)
        l = jnp.sum(p, axis=1, keepdims=True) + p_new
        is_branch = lax.broadcasted_iota(I32, (BR_ROWS, 1), 0) < len(BRANCHES)
        lse = jnp.where(is_branch, m + jnp.log(l), NEG)
        e = jnp.exp(lse - jnp.max(lse, axis=0, keepdims=True))
        wts = e / jnp.sum(e, axis=0, keepdims=True)
        pr, pnr, vtr, vnr = rnd(p), rnd(p_new), rnd(vt), rnd(vn)
        acc = jnp.zeros_like(vn)
        for br in range(len(BRANCHES)):
            rows = slice(br, br + 1)
            o_br = (jnp.sum(vtr * pr[rows], axis=1, keepdims=True) + pnr[rows] * vnr) / l[rows]
            acc = acc + wts[rows] * o_br
        o_ref[h] = acc
        ok_ref[h] = jnp.where(last, kn, pltpu.roll(kt, wb - 1, 1))
        ov_ref[h] = jnp.where(last, vn, pltpu.roll(vt, wb - 1, 1))


def _step_cache(qc, knc, vnc, ck_t, cv_t, bias_s, bias0, *, hb):
    b, _, _, wb = ck_t.shape
    col = pl.BlockSpec((None, hb, A_HD, 1), lambda i, j: (i, j, 0, 0))
    cache = pl.BlockSpec((None, hb, A_HD, wb), lambda i, j: (i, j, 0, 0))
    kern = functools.partial(_step_cache_kernel, hb=hb, wb=wb)
    return pl.pallas_call(
        kern,
        grid=(b, A_HEADS // hb),
        in_specs=[col, col, col, cache, cache,
                  pl.BlockSpec((hb, BR_ROWS, wb), lambda i, j: (j, 0, 0)),
                  pl.BlockSpec((hb, 1, 1), lambda i, j: (j, 0, 0))],
        out_specs=[col, cache, cache],
        out_shape=[jax.ShapeDtypeStruct((b, A_HEADS, A_HD, 1), F32),
                   jax.ShapeDtypeStruct(ck_t.shape, F32),
                   jax.ShapeDtypeStruct(cv_t.shape, F32)],
        compiler_params=_cparams(("parallel", "parallel")),
        name="step_cache",
    )(qc, knc, vnc, ck_t, cv_t, bias_s, bias0)


def _step_bias_table(rel_bias, wb):
    dist = wb - np.arange(wb)
    bkt = np.zeros((BR_ROWS, wb), np.int32)
    ok = np.zeros((BR_ROWS, wb), bool)
    for br, (w, d) in enumerate(BRANCHES):
        bkt[br] = _t5_bucket(dist)
        ok[br] = (dist <= w) & (dist % d == 0)
    tab = _bias_lookup(rel_bias, bkt, ok)
    bias0 = _bias_lookup(rel_bias, _t5_bucket(np.zeros((1, 1))), np.ones((1, 1), bool))
    return tab, bias0


def _mix_tail(pooled, attn, x, pw_ref, ps_ref, woa_ref, wop_ref):
    pb = jnp.dot(pooled.astype(BF16), pw_ref[...], preferred_element_type=F32) * ps_ref[...]
    y = jnp.dot(attn.astype(BF16), woa_ref[...], preferred_element_type=F32)
    y = y + jnp.dot(pb.astype(BF16), wop_ref[...], preferred_element_type=F32)
    return x + y


def _pick_by_group(vals):
    col = lax.broadcasted_iota(I32, (1, POOL_W), 1)
    out = vals[-1]
    for g in range(len(vals) - 2, -1, -1):
        out = jnp.where(col < (g + 1) * POOL_C, vals[g], out)
    return out


def _mix_prompt_kernel(u_ref, uh_ref, a_ref, x_ref, pw_ref, ps_ref, woa_ref, wop_ref, o_ref, *, tm):
    i = pl.program_id(1)
    u = u_ref[...]
    halo = jnp.where(i > 0, uh_ref[...], 0.0)
    e = jnp.concatenate([halo, u], 0)
    hr = halo.shape[0]
    sums, s, w = [], e, 1
    while w < POOL_WINDOWS[-1]:
        s = s + pltpu.roll(s, w, 0)
        w *= 2
        sums.append(s)
    pos = i * tm + lax.broadcasted_iota(I32, (tm, 1), 0)
    means = [sw[hr:] / jnp.minimum(wd, pos + 1).astype(F32) for sw, wd in zip(sums, POOL_WINDOWS)]
    pooled = _pick_by_group(means) - u
    o_ref[...] = _mix_tail(pooled, a_ref[...], x_ref[...], pw_ref, ps_ref, woa_ref, wop_ref)


def _mix_weight_specs():
    zero = lambda *_: (0, 0)
    return [
        pl.BlockSpec((POOL_W, POOL_W), zero),
        pl.BlockSpec((1, POOL_W), zero),
        pl.BlockSpec((A_W, D_MODEL), zero),
        pl.BlockSpec((POOL_W, D_MODEL), zero),
    ]


def _mix_prompt(z3, attn, x3, pw_bd, ps, woa, wop, *, tm):
    b, seq, _ = x3.shape
    hr = 16
    kern = functools.partial(_mix_prompt_kernel, tm=tm)
    return pl.pallas_call(
        kern,
        grid=(b, seq // tm),
        in_specs=[
            pl.BlockSpec((None, tm, POOL_W), lambda bi, i: (bi, i, 0)),
            pl.BlockSpec((None, hr, POOL_W), lambda bi, i: (bi, jnp.maximum(i * (tm // hr) - 1, 0), 0)),
            pl.BlockSpec((None, tm, A_W), lambda bi, i: (bi, i, 0)),
            pl.BlockSpec((None, tm, D_MODEL), lambda bi, i: (bi, i, 0)),
        ] + _mix_weight_specs(),
        out_specs=pl.BlockSpec((None, tm, D_MODEL), lambda bi, i: (bi, i, 0)),
        out_shape=jax.ShapeDtypeStruct((b, seq, D_MODEL), F32),
        compiler_params=_cparams(("parallel", "parallel")),
        name="mix_prompt",
    )(z3, z3, attn, x3, pw_bd, ps, woa, wop)


def _mix_step_kernel(sp_ref, u_ref, a_ref, x_ref, pw_ref, ps_ref, woa_ref, wop_ref, o_ref):
    u = u_ref[...]
    means, hist, j = [], u, POOL_HIST
    for w in POOL_WINDOWS:
        while j > POOL_HIST - (w - 1):
            j -= 1
            hist = hist + sp_ref[j]
        means.append(hist / float(w))
    pooled = _pick_by_group(means) - u
    o_ref[...] = _mix_tail(pooled, a_ref[...], x_ref[...], pw_ref, ps_ref, woa_ref, wop_ref)


def _mix_step(sp_t, us, attn, xs, pw_bd, ps, woa, wop):
    b = xs.shape[0]
    full = lambda shape: pl.BlockSpec(shape, lambda i: (0,) * len(shape))
    return pl.pallas_call(
        _mix_step_kernel,
        grid=(1,),
        in_specs=[full(sp_t.shape), full(us.shape), full(attn.shape), full(xs.shape)] + _mix_weight_specs(),
        out_specs=full((b, D_MODEL)),
        out_shape=jax.ShapeDtypeStruct((b, D_MODEL), F32),
        compiler_params=_cparams(("arbitrary",)),
        name="mix_step",
    )(sp_t, us, attn, xs, pw_bd, ps, woa, wop)


def _first_argmax(vals, lane):
    mx = jnp.max(vals, -1, keepdims=True)
    idx = jnp.min(jnp.where(vals == mx, lane, LANES), -1, keepdims=True)
    return mx, idx


def _cross_kernel(x_ref, g_ref, wq_ref, cqg_ref, mk_ref, mv_ref, wo_ref, gf_ref, wr_ref, br_ref,
                  y_ref, h_ref, rt_ref, hist_ref):
    x = x_ref[...]
    tm = x.shape[0]
    hq = _rms_rows(x, g_ref[...]).astype(BF16)
    q = jnp.dot(hq, wq_ref[...], preferred_element_type=F32)
    outs = []
    for h in range(X_HEADS):
        sl = slice(h * X_HD, (h + 1) * X_HD)
        qh = _rms_rows(q[:, sl], cqg_ref[...]).astype(BF16)
        kh = mk_ref[:, sl].astype(BF16)
        vh = mv_ref[:, sl].astype(BF16)
        s = lax.dot_general(qh, kh, (((1,), (1,)), ((), ())), preferred_element_type=F32) * X_SCALE
        p = jnp.exp(s - jnp.max(s, -1, keepdims=True))
        p = p / jnp.sum(p, -1, keepdims=True)
        outs.append(jnp.dot(p.astype(BF16), vh, preferred_element_type=F32))
    o = jnp.concatenate(outs, -1).astype(BF16)
    y = x + jnp.dot(o, wo_ref[...], preferred_element_type=F32)
    y_ref[...] = y

    hf = _rms_rows(y, gf_ref[...])
    h_bf = hf.astype(BF16)
    bits = lax.bitcast_convert_type(h_bf.astype(F32), U32)
    h_ref[...] = bits[:, :HALF] | (bits[:, HALF:] >> 16)

    lg = jnp.dot(h_bf, wr_ref[...], preferred_element_type=F32) + br_ref[...]

    lane = lax.broadcasted_iota(I32, lg.shape, 1)
    lgg = jnp.where(lane < N_GROUPS, lg, NEG)
    mg, g_sel = _first_argmax(lgg, lane)
    p_grp = 1.0 / jnp.sum(jnp.exp(lgg - mg), -1, keepdims=True)
    e_lo = N_GROUPS + E_PER_GROUP * g_sel
    le = jnp.where(jnp.logical_and(lane >= e_lo, lane < e_lo + E_PER_GROUP), lg, NEG)
    v0, i0 = _first_argmax(le, lane)
    v1, i1 = _first_argmax(jnp.where(lane == i0, NEG, le), lane)
    e1w = jnp.exp(v1 - v0)
    g0 = p_grp / (1.0 + e1w)
    g1 = p_grp * e1w / (1.0 + e1w)
    e0 = i0 - N_GROUPS
    e1 = i1 - N_GROUPS

    cnt = jnp.logical_or(lane == e0, lane == e1).astype(F32)
    if tm > 1:
        ri = lax.broadcasted_iota(I32, (tm, tm), 0)
        ci = lax.broadcasted_iota(I32, (tm, tm), 1)
        before = (ci < ri).astype(BF16)
        pre = jnp.dot(before, cnt.astype(BF16), preferred_element_type=F32)
        r0 = jnp.sum(jnp.where(lane == e0, pre, 0.0), -1, keepdims=True)
        r1 = jnp.sum(jnp.where(lane == e1, pre, 0.0), -1, keepdims=True)
    else:
        r0 = r1 = jnp.zeros_like(g0)
    hist_ref[...] = jnp.sum(cnt, axis=0, keepdims=True)

    rec = jnp.zeros(lg.shape, F32)
    for ln, val in ((RT_E0, e0.astype(F32)), (RT_E1, e1.astype(F32)), (RT_G0, g0), (RT_G1, g1),
                    (RT_R0, r0), (RT_R1, r1)):
        rec = jnp.where(lane == ln, val, rec)
    rt_ref[...] = rec


def _cross(x3, mk, mv, g, wq, cqg, wo, gf, wr, brt, *, tm):
    b, seq, _ = x3.shape
    nt = seq // tm
    zero = lambda bi, i: (0, 0)
    tok = lambda w: pl.BlockSpec((None, tm, w), lambda bi, i: (bi, i, 0))
    mem = pl.BlockSpec((None, N_MEM, X_W), lambda bi, i: (bi, 0, 0))
    return pl.pallas_call(
        _cross_kernel,
        grid=(b, nt),
        in_specs=[
            tok(D_MODEL),
            pl.BlockSpec((1, D_MODEL), zero),
            pl.BlockSpec((D_MODEL, X_W), zero),
            pl.BlockSpec((1, X_HD), zero),
            mem, mem,
            pl.BlockSpec((X_W, D_MODEL), zero),
            pl.BlockSpec((1, D_MODEL), zero),
            pl.BlockSpec((D_MODEL, LANES), zero),
            pl.BlockSpec((1, LANES), zero),
        ],
        out_specs=[tok(D_MODEL), tok(HALF), tok(LANES),
                   pl.BlockSpec((None, None, 1, LANES), lambda bi, i: (bi, i, 0, 0))],
        out_shape=[
            jax.ShapeDtypeStruct((b, seq, D_MODEL), F32),
            jax.ShapeDtypeStruct((b, seq, HALF), U32),
            jax.ShapeDtypeStruct((b, seq, LANES), F32),
            jax.ShapeDtypeStruct((b, nt, 1, LANES), F32),
        ],
        compiler_params=_cparams(("parallel", "parallel")),
        name="cross_b%d" % b,
    )(x3, g, wq, cqg, mk, mv, wo, gf, wr, brt)


def _invmap_kernel(dest_ref, pend_ref, src_ref, *, n_assign, n_tok, n_blocks, blk):
    def fill(base):
        def body(r, c):
            p = base + r
            src_ref[p] = TOP_K * n_tok + ((p // blk) % 2) * blk + r
            return c
        lax.fori_loop(0, blk, body, 0, unroll=8)

    for e in range(N_EXPERTS):
        fill(jnp.maximum(pend_ref[e] - blk, 0))

    def fill_block(j, c):
        fill(j * blk)
        return c

    lax.fori_loop(pend_ref[N_EXPERTS - 1] // blk, n_blocks + 1, fill_block, 0)

    def put(a, c):
        src_ref[dest_ref[a]] = a
        return c

    lax.fori_loop(0, n_assign, put, 0, unroll=8)


def _invmap(dest, pend, *, n_tok, n_blocks, blk):
    assert n_blocks % 2 == 1, "the spare block must have odd parity"
    smem = pl.BlockSpec(memory_space=pltpu.SMEM)
    kern = functools.partial(_invmap_kernel, n_assign=dest.shape[0], n_tok=n_tok, n_blocks=n_blocks, blk=blk)
    return pl.pallas_call(
        kern,
        in_specs=[smem, smem],
        out_specs=smem,
        out_shape=jax.ShapeDtypeStruct(((n_blocks + 1) * blk,), I32),
        name="moe_invmap",
    )(dest, pend)


def _expert_kernel(be_ref, nb_ref, src_ref, h_ref, wg_ref, wu_ref, wd_ref, y2_ref,
                   xbuf0, xbuf1, obuf0, obuf1, wg_bf, wu_bf, wd_bf, gsem, ssem, *, blk):
    i = pl.program_id(0)
    nb = nb_ref[0]
    n_rows = pl.num_programs(0) * blk
    xbufs, obufs = (xbuf0, xbuf1), (obuf0, obuf1)

    def gather_copy(tok, r, slot):
        return pltpu.make_async_copy(h_ref.at[pl.ds(tok, 1)], xbufs[slot].at[pl.ds(r, 1)], gsem.at[slot])

    def scatter_copy(r, dst, slot):
        return pltpu.make_async_copy(obufs[slot].at[pl.ds(r, 1)], y2_ref.at[pl.ds(dst, 1)], ssem.at[slot])

    def gather_start(base, slot):
        for r in range(blk):
            gather_copy(src_ref[base + r] >> 1, r, slot).start()

    def gather_wait(slot):
        for r in range(blk):
            gather_copy(0, r, slot).wait()

    def scatter_start(base, slot):
        for r in range(blk):
            scatter_copy(r, src_ref[base + r], slot).start()

    def scatter_wait(slot):
        for r in range(blk):
            scatter_copy(r, 0, slot).wait()

    @pl.when(i == 0)
    def _():
        obuf1[...] = jnp.zeros(obuf1.shape, F32)
        even_dump = y2_ref.shape[0] - 2 * blk
        for r in range(blk):
            scatter_copy(r, even_dump + r, 1).start()
        scatter_wait(1)
        gather_start(0, 0)

    def step(slot):
        other = 1 - slot
        gather_wait(slot)
        changed = jnp.logical_or(i == 0, be_ref[i] != be_ref[jnp.maximum(i - 1, 0)])

        @pl.when(changed)
        def _():
            wg_bf[...] = wg_ref[...].astype(BF16)
            wu_bf[...] = wu_ref[...].astype(BF16)
            wd_bf[...] = wd_ref[...].astype(BF16)

        gather_start(jnp.minimum(i + 1, nb - 1) * blk, other)
        scatter_start(jnp.where(i > 0, (i - 1) * blk, n_rows), other)
        xu = xbufs[slot][...]
        xa = lax.bitcast_convert_type(xu & jnp.uint32(0xFFFF0000), F32).astype(BF16)
        xb = lax.bitcast_convert_type(xu << 16, F32).astype(BF16)
        x = jnp.concatenate([xa, xb], -1)
        a = jax.nn.silu(jnp.dot(x, wg_bf[...], preferred_element_type=F32))
        b = jnp.dot(x, wu_bf[...], preferred_element_type=F32)
        obufs[slot][...] = jnp.dot((a * b).astype(BF16), wd_bf[...], preferred_element_type=F32)
        scatter_wait(other)

        @pl.when(i == nb - 1)
        def _():
            scatter_start(i * blk, slot)
            scatter_wait(slot)
            gather_wait(other)

    @pl.when(jnp.logical_and(i < nb, i % 2 == 0))
    def _():
        step(0)

    @pl.when(jnp.logical_and(i < nb, i % 2 == 1))
    def _():
        step(1)


def _experts(blk_e, n_used, src, h_all, w_gate, w_up, w_down, *, blk):
    n_tok = h_all.shape[0] - blk
    wspec = lambda k, n: pl.BlockSpec((None, k, n), lambda i, be, nb, src: (be[i], 0, 0))
    return pl.pallas_call(
        functools.partial(_expert_kernel, blk=blk),
        grid_spec=pltpu.PrefetchScalarGridSpec(
            num_scalar_prefetch=3,
            grid=(blk_e.shape[0],),
            in_specs=[pl.BlockSpec(memory_space=pl.ANY),
                      wspec(D_MODEL, D_EXPERT), wspec(D_MODEL, D_EXPERT), wspec(D_EXPERT, D_MODEL)],
            out_specs=pl.BlockSpec(memory_space=pl.ANY),
            scratch_shapes=[
                pltpu.VMEM((blk, HALF), U32), pltpu.VMEM((blk, HALF), U32),
                pltpu.VMEM((blk, D_MODEL), F32), pltpu.VMEM((blk, D_MODEL), F32),
                pltpu.VMEM((D_MODEL, D_EXPERT), BF16),
                pltpu.VMEM((D_MODEL, D_EXPERT), BF16),
                pltpu.VMEM((D_EXPERT, D_MODEL), BF16),
                pltpu.SemaphoreType.DMA((2,)), pltpu.SemaphoreType.DMA((2,)),
            ],
        ),
        out_shape=jax.ShapeDtypeStruct((TOP_K * n_tok + 2 * blk, D_MODEL), F32),
        compiler_params=_cparams(("arbitrary",)),
        name="experts",
    )(blk_e, n_used, src, h_all, w_gate, w_up, w_down)


def _combine_kernel(x_ref, rt_ref, y_ref, o_ref):
    rt = rt_ref[...]
    g0 = rt[:, RT_G0:RT_G0 + 1]
    g1 = rt[:, RT_G1:RT_G1 + 1]
    o_ref[...] = x_ref[...] + (g0 * y_ref[:, :D_MODEL] + g1 * y_ref[:, D_MODEL:])


def _combine(x, rt, y2w, *, tm, row0):
    n = x.shape[0]
    off = row0 // tm
    assert off * tm == row0
    return pl.pallas_call(
        _combine_kernel,
        grid=(n // tm,),
        in_specs=[pl.BlockSpec((tm, D_MODEL), lambda i: (i, 0)),
                  pl.BlockSpec((tm, LANES), lambda i: (i, 0)),
                  pl.BlockSpec((tm, TOP_K * D_MODEL), lambda i: (i + off, 0))],
        out_specs=pl.BlockSpec((tm, D_MODEL), lambda i: (i, 0)),
        out_shape=jax.ShapeDtypeStruct((n, D_MODEL), F32),
        compiler_params=_cparams(("parallel",)),
        name="combine_n%d" % n,
    )(x, rt, y2w)


def _dispatch_plan(rt_tiles, hist_tiles, tile_sizes, *, blk):
    hist = jnp.concatenate(hist_tiles, axis=0)[:, :N_EXPERTS].astype(I32)
    counts = jnp.sum(hist, axis=0)
    padded = (counts + blk - 1) // blk * blk
    pend = jnp.cumsum(padded)
    pstart = pend - padded
    tile_base = pstart[None, :] + jnp.cumsum(hist, axis=0) - hist
    ids = jnp.arange(N_EXPERTS, dtype=I32)
    dests, t0 = [], 0
    for rt, ht, ts in zip(rt_tiles, hist_tiles, tile_sizes):
        nt = ht.shape[0]
        tb = jnp.repeat(tile_base[t0:t0 + nt], ts, axis=0)
        t0 += nt
        e = rt[:, RT_E0:RT_E1 + 1].astype(I32)
        r = rt[:, RT_R0:RT_R1 + 1].astype(I32)
        base = jnp.sum(jnp.where(e[:, :, None] == ids[None, None, :], tb[:, None, :], 0), axis=-1)
        dests.append((base + r).reshape(-1))
    return dests, pend


def kernel(x_prompt, x_sample, cache_win_k, cache_win_v, state_pool, cache_mem_k, cache_mem_v,
           mem_prompt, rel_bias, norm_mix_g, w_in, q_norm_g, k_norm_g, pool_w, pool_scale, w_out,
           norm_cross_g, norm_mem_g, w_cq, w_ck, w_cv, cq_norm_g, ck_norm_g, w_co,
           norm_ffn_g, w_rg, b_rg, w_re, b_re, w_e_gate, w_e_up, w_e_down):
    assert norm_mix_g.shape[0] == 1, "single-layer step"
    bp, seq, d = x_prompt.shape
    bs = x_sample.shape[0]
    wb = cache_win_k.shape[2]
    np_tok = bp * seq

    w_in_l = w_in[0]
    w_in_bf = jnp.concatenate([w_in_l[:, 3 * A_W:], w_in_l[:, :3 * A_W]], axis=1).astype(BF16)
    cg_in = jnp.concatenate([jnp.ones((POOL_W,), F32), jnp.tile(q_norm_g[0], A_HEADS) * A_SCALE,
                             jnp.tile(k_norm_g[0], A_HEADS), jnp.ones((A_W,), F32)])[None, :]
    g_mix = norm_mix_g[0][None, :]
    ng = len(POOL_WINDOWS)
    pw_bd = jnp.concatenate(
        [jnp.pad(pool_w[0, g], ((0, 0), (g * POOL_C, (ng - 1 - g) * POOL_C))) for g in range(ng)],
        axis=0).astype(BF16)
    ps = pool_scale[0][None, :]
    woa = w_out[0, :A_W].astype(BF16)
    wop = w_out[0, A_W:].astype(BF16)
    w_mem_bf = jnp.concatenate([w_ck[0], w_cv[0]], axis=1).astype(BF16)
    cg_mem = jnp.concatenate([jnp.tile(ck_norm_g[0], X_HEADS), jnp.ones((X_W,), F32)])[None, :]
    wq_bf = w_cq[0].astype(BF16)
    wo_bf = w_co[0].astype(BF16)
    wr_bf = jnp.pad(jnp.concatenate([w_rg[0], w_re[0]], axis=1),
                    ((0, 0), (0, LANES - N_GROUPS - N_EXPERTS))).astype(BF16)
    br = jnp.pad(jnp.concatenate([b_rg[0], b_re[0]]), (0, LANES - N_GROUPS - N_EXPERTS))[None, :]
    bias_tab = _prompt_bias_table(rel_bias)
    bias_s, bias0 = _step_bias_table(rel_bias, wb)

    proj_in = functools.partial(_proj, g=g_mix, w_bf=w_in_bf, colgain=cg_in,
                                norm_lo=Q_OFF // MXU_N, norm_hi=V_OFF // MXU_N, group=A_HD)

    zp = proj_in(x_prompt.reshape(np_tok, d), tm=1024, name="proj_in_prompt")
    z3 = zp.reshape(bp, seq, IN_W)
    attn_p = _attn_prompt(z3, bias_tab)
    xp1 = _mix_prompt(z3, attn_p, x_prompt, pw_bd, ps, woa, wop, tm=256)

    xs2 = x_sample.reshape(bs, d)
    zs = proj_in(xs2, tm=bs, name="proj_in_step")
    col = lambda off: zs[:, off:off + A_W].reshape(bs, A_HEADS, A_HD, 1)
    us = zs[:, :POOL_W]
    ck_t = jnp.transpose(cache_win_k[0], (0, 2, 3, 1))
    cv_t = jnp.transpose(cache_win_v[0], (0, 2, 3, 1))
    attn_s, nk_t, nv_t = _step_cache(col(Q_OFF), col(K_OFF), col(V_OFF), ck_t, cv_t, bias_s, bias0, hb=4)
    sp_t = jnp.transpose(state_pool[0], (1, 0, 2))
    xs1 = _mix_step(sp_t, us, attn_s.reshape(bs, A_W).astype(BF16), xs2, pw_bd, ps, woa, wop)

    zm = _proj(mem_prompt.reshape(bp * N_MEM, d), norm_mem_g[0][None, :], w_mem_bf, cg_mem,
               tm=512, norm_lo=0, norm_hi=X_W // MXU_N, group=X_HD, name="mem_kv")
    mk_p = zm[:, :X_W].reshape(bp, N_MEM, X_W)
    mv_p = zm[:, X_W:].reshape(bp, N_MEM, X_W)

    cross = functools.partial(_cross, g=norm_cross_g[0][None, :], wq=wq_bf, cqg=cq_norm_g[0][None, :],
                              wo=wo_bf, gf=norm_ffn_g[0][None, :], wr=wr_bf, brt=br)
    tm_x = 512
    xp2, hp, rtp, histp = cross(xp1, mk_p, mv_p, tm=tm_x)
    xs2b, hs, rts, hists = cross(xs1.reshape(bs, 1, d), cache_mem_k[0].reshape(bs, N_MEM, X_W),
                                 cache_mem_v[0].reshape(bs, N_MEM, X_W), tm=1)

    blk = 256
    n_all = np_tok + bs
    n_blocks = -(-(n_all * TOP_K + N_EXPERTS * (blk - 1)) // blk) | 1
    rtp2, rts2 = rtp.reshape(np_tok, LANES), rts.reshape(bs, LANES)
    dests, pend = _dispatch_plan(
        [rtp2, rts2], [histp.reshape(-1, LANES), hists.reshape(-1, LANES)], [tm_x, 1], blk=blk)
    starts = jnp.arange(n_blocks, dtype=I32) * blk
    blk_e = jnp.minimum(jnp.sum((pend[None, :] <= starts[:, None]).astype(I32), axis=1), N_EXPERTS - 1)
    n_used = (pend[-1:] // blk).astype(I32)
    src = _invmap(jnp.concatenate(dests), pend, n_tok=n_all, n_blocks=n_blocks, blk=blk)
    h_all = jnp.concatenate([hp.reshape(np_tok, HALF), hs.reshape(bs, HALF), jnp.zeros((blk, HALF), U32)])
    y2 = _experts(blk_e, n_used, src, h_all, w_e_gate[0], w_e_up[0], w_e_down[0], blk=blk)
    y2w = y2.reshape(-1, TOP_K * d)
    y_prompt = _combine(xp2.reshape(np_tok, d), rtp2, y2w, tm=256, row0=0).reshape(bp, seq, d)
    y_sample = _combine(xs2b.reshape(bs, d), rts2, y2w, tm=bs, row0=np_tok).reshape(bs, 1, d)

    win_k_p = z3[:, seq - min(MAX_DISTANCE, seq):, K_OFF:V_OFF].reshape(1, bp, -1, A_HEADS, A_HD)
    win_v_p = z3[:, seq - min(MAX_DISTANCE, seq):, V_OFF:].reshape(1, bp, -1, A_HEADS, A_HD)
    win_k_s = jnp.transpose(nk_t, (0, 3, 1, 2))[None]
    win_v_s = jnp.transpose(nv_t, (0, 3, 1, 2))[None]
    pool_p = z3[:, seq - POOL_HIST:, :POOL_W][None]
    pool_s = jnp.transpose(jnp.concatenate([sp_t[1:], us[None]], axis=0), (1, 0, 2))[None]
    mem_k_p = mk_p.reshape(1, bp, N_MEM, X_HEADS, X_HD)
    mem_v_p = mv_p.reshape(1, bp, N_MEM, X_HEADS, X_HD)
    return (y_prompt, y_sample, win_k_p, win_v_p, win_k_s, win_v_s, pool_p, pool_s, mem_k_p, mem_v_p)
```

```python
import functools

import numpy as np
import jax
import jax.numpy as jnp
from jax import lax
from jax.experimental import pallas as pl
from jax.experimental.pallas import tpu as pltpu

F32 = jnp.float32
BF16 = jnp.bfloat16
U32 = jnp.uint32
I32 = jnp.int32

D_MODEL = 2048
A_HEADS = 12
A_HD = 64
A_W = A_HEADS * A_HD
BRANCHES = ((128, 1), (512, 4), (2048, 16))
NK = 128
NUM_BUCKETS = 32
MAX_DISTANCE = 2048
POOL_WINDOWS = (2, 4, 8, 16)
POOL_W = D_MODEL - A_W
POOL_C = POOL_W // len(POOL_WINDOWS)
POOL_HIST = 15
IN_W = 3 * A_W + POOL_W
N_MEM = 256
X_HEADS = 4
X_HD = 128
X_W = X_HEADS * X_HD
X_SCALE = X_HD ** -0.5
A_SCALE = A_HD ** -0.5
N_GROUPS = 4
E_PER_GROUP = 8
N_EXPERTS = N_GROUPS * E_PER_GROUP
TOP_K = 2
D_EXPERT = 512
EPS = 1e-6
NEG = -1e30

LANES = 128
MXU_N = 256
VMEM_LIMIT = 52 * 1024 * 1024
HALF = D_MODEL // 2

U_OFF, Q_OFF, K_OFF, V_OFF = 0, POOL_W, POOL_W + A_W, POOL_W + 2 * A_W

RT_E0, RT_E1, RT_G0, RT_G1, RT_R0, RT_R1 = range(6)


def _cparams(sem):
    return pltpu.CompilerParams(dimension_semantics=sem, vmem_limit_bytes=VMEM_LIMIT)


def _rms_rows(x, g):
    ms = jnp.mean(x * x, axis=-1, keepdims=True)
    return x * lax.rsqrt(ms + EPS) * g


def _t5_bucket(dist):
    n = np.asarray(dist, dtype=np.int64)
    exact = NUM_BUCKETS // 2
    ratio = np.log(np.maximum(n, 1) / exact) / np.log(MAX_DISTANCE / exact)
    large = np.minimum(exact + (ratio * (NUM_BUCKETS - exact)).astype(np.int64), NUM_BUCKETS - 1)
    return np.where(n < exact, n, large).astype(np.int32)


def _bias_lookup(rel_bias, buckets, ok):
    buckets = np.asarray(buckets, np.int32)
    onehot = (jnp.arange(NUM_BUCKETS, dtype=I32)[:, None] == jnp.asarray(buckets.reshape(1, -1))).astype(F32)
    vals = jnp.einsum("bh,bn->hn", rel_bias.astype(F32), onehot, precision=lax.Precision.HIGHEST)
    vals = jnp.where(jnp.asarray(np.asarray(ok).reshape(1, -1)), vals, NEG)
    return vals.reshape((A_HEADS,) + buckets.shape)


def _proj_kernel(x_ref, g_ref, w_ref, cg_ref, o_ref, h_scr, *, norm_lo, norm_hi, group):
    j = pl.program_id(1)

    @pl.when(j == 0)
    def _():
        h_scr[...] = _rms_rows(x_ref[...], g_ref[...]).astype(BF16)

    z = jnp.dot(h_scr[...], w_ref[...], preferred_element_type=F32)
    is_norm = jnp.logical_and(j >= norm_lo, j < norm_hi)

    @pl.when(is_norm)
    def _():
        lo_mask = lax.broadcasted_iota(I32, (1, LANES), 1) < group
        outs = []
        for c in range(z.shape[1] // LANES):
            zc = z[:, c * LANES:(c + 1) * LANES]
            zz = zc * zc
            if group == LANES:
                r = lax.rsqrt(jnp.sum(zz, -1, keepdims=True) * (1.0 / group) + EPS)
            else:
                lo = jnp.sum(jnp.where(lo_mask, zz, 0.0), -1, keepdims=True) * (1.0 / group)
                hi = jnp.sum(jnp.where(lo_mask, 0.0, zz), -1, keepdims=True) * (1.0 / group)
                r = jnp.where(lo_mask, lax.rsqrt(lo + EPS), lax.rsqrt(hi + EPS))
            outs.append(zc * r)
        o_ref[...] = jnp.concatenate(outs, -1) * cg_ref[...]

    @pl.when(jnp.logical_not(is_norm))
    def _():
        o_ref[...] = z


def _proj(x, g, w_bf, colgain, *, tm, norm_lo, norm_hi, group, name):
    n, d = x.shape
    width = w_bf.shape[1]
    kern = functools.partial(_proj_kernel, norm_lo=norm_lo, norm_hi=norm_hi, group=group)
    return pl.pallas_call(
        kern,
        grid=(n // tm, width // MXU_N),
        in_specs=[
            pl.BlockSpec((tm, d), lambda i, j: (i, 0)),
            pl.BlockSpec((1, d), lambda i, j: (0, 0)),
            pl.BlockSpec((d, MXU_N), lambda i, j: (0, j)),
            pl.BlockSpec((1, MXU_N), lambda i, j: (0, j)),
        ],
        out_specs=pl.BlockSpec((tm, MXU_N), lambda i, j: (i, j)),
        out_shape=jax.ShapeDtypeStruct((n, width), F32),
        scratch_shapes=[pltpu.VMEM((tm, d), BF16)],
        compiler_params=_cparams(("parallel", "arbitrary")),
        name=name,
    )(x, g, w_bf, colgain)


def _attn_prompt_kernel(q_ref, k_ref, v_ref, bias_ref, o_ref, m_scr, l_scr, acc_scr, *, seq):
    lo = lax.broadcasted_iota(I32, (NK, LANES), 1) < A_HD

    for br, (_, d) in enumerate(BRANCHES):
        nblk = seq // (d * NK)

        def rows(start, d=d):
            return pl.ds(start, NK, stride=d) if d > 1 else pl.ds(start, NK)

        for r in range(d):
            for qb in range(nblk):
                cur = rows(r + d * NK * qb)
                dst = pl.ds((r * nblk + qb) * NK, NK)
                qf = q_ref[cur, :]
                if qb > 0:
                    prev = rows(r + d * NK * (qb - 1))
                    kk = jnp.concatenate([k_ref[prev, :], k_ref[cur, :]], 0)
                    vv = jnp.concatenate([v_ref[prev, :], v_ref[cur, :]], 0)
                    off = 0
                else:
                    kk, vv, off = k_ref[cur, :], v_ref[cur, :], NK
                kk = kk.astype(BF16)
                vv = vv.astype(BF16)
                ms, ls, pvs = [], [], []
                for h in range(2):
                    qh = jnp.where(lo if h == 0 else jnp.logical_not(lo), qf, 0.0).astype(BF16)
                    s = lax.dot_general(qh, kk, (((1,), (1,)), ((), ())), preferred_element_type=F32)
                    s = s + bias_ref[br, h, :, off:2 * NK]
                    m = jnp.max(s, -1, keepdims=True)
                    p = jnp.exp(s - m)
                    ms.append(m)
                    ls.append(jnp.sum(p, -1, keepdims=True))
                    pvs.append(jnp.dot(p.astype(BF16), vv, preferred_element_type=F32))
                acc_scr[br, dst, :] = jnp.where(lo, pvs[0], pvs[1])
                m_scr[br, dst, :] = jnp.where(lo, ms[0], ms[1])
                l_scr[br, dst, :] = jnp.where(lo, ls[0], ls[1])

    d4, d16 = BRANCHES[1][1], BRANCHES[2][1]
    for r in range(d16):
        for qb in range(seq // (d16 * NK)):
            nat = pl.ds(r + d16 * NK * qb, NK, stride=d16)
            views = (nat,
                     pl.ds((r % d4) * (seq // d4) + r // d4 + (d16 // d4) * NK * qb, NK, stride=d16 // d4),
                     pl.ds(r * (seq // d16) + NK * qb, NK))
            m = [m_scr[i, v, :] for i, v in enumerate(views)]
            mx = jnp.maximum(jnp.maximum(m[0], m[1]), m[2])
            w = [jnp.exp(mi - mx) for mi in m]
            num = sum(wi * acc_scr[i, v, :] for i, (wi, v) in enumerate(zip(w, views)))
            den = sum(wi * l_scr[i, v, :] for i, (wi, v) in enumerate(zip(w, views)))
            o_ref[nat, :] = num / den


def _attn_prompt(z3, bias_tab):
    b, seq, _ = z3.shape
    assert seq % BRANCHES[-1][0] == 0, "sequence must be a multiple of the widest window"
    nhp = A_W // LANES
    nbr = len(BRANCHES)
    qo, ko, vo = Q_OFF // LANES, K_OFF // LANES, V_OFF // LANES
    kern = functools.partial(_attn_prompt_kernel, seq=seq)
    return pl.pallas_call(
        kern,
        grid=(b, nhp),
        in_specs=[
            pl.BlockSpec((None, seq, LANES), lambda i, hp: (i, 0, qo + hp)),
            pl.BlockSpec((None, seq, LANES), lambda i, hp: (i, 0, ko + hp)),
            pl.BlockSpec((None, seq, LANES), lambda i, hp: (i, 0, vo + hp)),
            pl.BlockSpec((len(BRANCHES), None, 2, NK, 2 * NK), lambda i, hp: (0, hp, 0, 0, 0)),
        ],
        out_specs=pl.BlockSpec((None, seq, LANES), lambda i, hp: (i, 0, hp)),
        out_shape=jax.ShapeDtypeStruct((b, seq, A_W), F32),
        scratch_shapes=[pltpu.VMEM((nbr, seq, LANES), F32)] * 3,
        compiler_params=_cparams(("parallel", "parallel")),
        name="attn_prompt",
    )(z3, z3, z3, bias_tab)


def _prompt_bias_table(rel_bias):
    qi = np.arange(NK)[:, None]
    kj = np.arange(2 * NK)[None, :]
    rel = qi + NK - kj
    ok = (rel >= 0) & (rel <= NK)
    bkt = np.stack([_t5_bucket(np.clip(rel, 0, None) * d) for _, d in BRANCHES])
    tab = _bias_lookup(rel_bias, bkt, np.broadcast_to(ok, bkt.shape))
    return jnp.transpose(tab, (1, 0, 2, 3)).reshape(len(BRANCHES), A_HEADS // 2, 2, NK, 2 * NK)


BR_ROWS = 8


def _step_cache_kernel(q_ref, kn_ref, vn_ref, ck_ref, cv_ref, bias_ref, b0_ref, o_ref, ok_ref, ov_ref,
                       *, hb, wb):
    last = lax.broadcasted_iota(I32, (1, wb), 1) == wb - 1
    rnd = lambda t: t.astype(BF16).astype(F32)
    for h in range(hb):
        q = rnd(q_ref[h])
        kt, vt = ck_ref[h], cv_ref[h]
        kn, vn = kn_ref[h], vn_ref[h]
        s = jnp.sum(rnd(kt) * q, axis=0, keepdims=True)
        sb = s + bias_ref[h]
        s_new = jnp.sum(rnd(kn) * q, axis=0, keepdims=True) + b0_ref[h]
        m = jnp.maximum(jnp.max(sb, axis=1, keepdims=True), s_new)
        p = jnp.exp(sb - m)
        p_new = jnp.exp(s_new - m)
        l = jnp.sum(p, axis=1, keepdims=True) + p_new
        is_branch = lax.broadcasted_iota(I32, (BR_ROWS, 1), 0) < len(BRANCHES)
        lse = jnp.where(is_branch, m + jnp.log(l), NEG)
        e = jnp.exp(lse - jnp.max(lse, axis=0, keepdims=True))
        wts = e / jnp.sum(e, axis=0, keepdims=True)
        pr, pnr, vtr, vnr = rnd(p), rnd(p_new), rnd(vt), rnd(vn)
        acc = jnp.zeros_like(vn)
        for br in range(len(BRANCHES)):
            rows = slice(br, br + 1)
            o_br = (jnp.sum(vtr * pr[rows], axis=1, keepdims=True) + pnr[rows] * vnr) / l[rows]
            acc = acc + wts[rows] * o_br
        o_ref[h] = acc
        ok_ref[h] = jnp.where(last, kn, pltpu.roll(kt, wb - 1, 1))
        ov_ref[h] = jnp.where(last, vn, pltpu.roll(vt, wb - 1, 1))


def _step_cache(qc, knc, vnc, ck_t, cv_t, bias_s, bias0, *, hb):
    b, _, _, wb = ck_t.shape
    col = pl.BlockSpec((None, hb, A_HD, 1), lambda i, j: (i, j, 0, 0))
    cache = pl.BlockSpec((None, hb, A_HD, wb), lambda i, j: (i, j, 0, 0))
    kern = functools.partial(_step_cache_kernel, hb=hb, wb=wb)
    return pl.pallas_call(
        kern,
        grid=(b, A_HEADS // hb),
        in_specs=[col, col, col, cache, cache,
                  pl.BlockSpec((hb, BR_ROWS, wb), lambda i, j: (j, 0, 0)),
                  pl.BlockSpec((hb, 1, 1), lambda i, j: (j, 0, 0))],
        out_specs=[col, cache, cache],
        out_shape=[jax.ShapeDtypeStruct((b, A_HEADS, A_HD, 1), F32),
                   jax.ShapeDtypeStruct(ck_t.shape, F32),
                   jax.ShapeDtypeStruct(cv_t.shape, F32)],
        compiler_params=_cparams(("parallel", "parallel")),
        name="step_cache",
    )(qc, knc, vnc, ck_t, cv_t, bias_s, bias0)


def _step_bias_table(rel_bias, wb):
    dist = wb - np.arange(wb)
    bkt = np.zeros((BR_ROWS, wb), np.int32)
    ok = np.zeros((BR_ROWS, wb), bool)
    for br, (w, d) in enumerate(BRANCHES):
        bkt[br] = _t5_bucket(dist)
        ok[br] = (dist <= w) & (dist % d == 0)
    tab = _bias_lookup(rel_bias, bkt, ok)
    bias0 = _bias_lookup(rel_bias, _t5_bucket(np.zeros((1, 1))), np.ones((1, 1), bool))
    return tab, bias0


def _mix_tail(pooled, attn, x, pw_ref, ps_ref, woa_ref, wop_ref):
    pb = jnp.dot(pooled.astype(BF16), pw_ref[...], preferred_element_type=F32) * ps_ref[...]
    y = jnp.dot(attn.astype(BF16), woa_ref[...], preferred_element_type=F32)
    y = y + jnp.dot(pb.astype(BF16), wop_ref[...], preferred_element_type=F32)
    return x + y


def _pick_by_group(vals):
    col = lax.broadcasted_iota(I32, (1, POOL_W), 1)
    out = vals[-1]
    for g in range(len(vals) - 2, -1, -1):
        out = jnp.where(col < (g + 1) * POOL_C, vals[g], out)
    return out


def _mix_prompt_kernel(u_ref, uh_ref, a_ref, x_ref, pw_ref, ps_ref, woa_ref, wop_ref, o_ref, *, tm):
    i = pl.program_id(1)
    u = u_ref[...]
    halo = jnp.where(i > 0, uh_ref[...], 0.0)
    e = jnp.concatenate([halo, u], 0)
    hr = halo.shape[0]
    sums, s, w = [], e, 1
    while w < POOL_WINDOWS[-1]:
        s = s + pltpu.roll(s, w, 0)
        w *= 2
        sums.append(s)
    pos = i * tm + lax.broadcasted_iota(I32, (tm, 1), 0)
    means = [sw[hr:] / jnp.minimum(wd, pos + 1).astype(F32) for sw, wd in zip(sums, POOL_WINDOWS)]
    pooled = _pick_by_group(means) - u
    o_ref[...] = _mix_tail(pooled, a_ref[...], x_ref[...], pw_ref, ps_ref, woa_ref, wop_ref)


def _mix_weight_specs():
    zero = lambda *_: (0, 0)
    return [
        pl.BlockSpec((POOL_W, POOL_W), zero),
        pl.BlockSpec((1, POOL_W), zero),
        pl.BlockSpec((A_W, D_MODEL), zero),
        pl.BlockSpec((POOL_W, D_MODEL), zero),
    ]


def _mix_prompt(z3, attn, x3, pw_bd, ps, woa, wop, *, tm):
    b, seq, _ = x3.shape
    hr = 16
    kern = functools.partial(_mix_prompt_kernel, tm=tm)
    return pl.pallas_call(
        kern,
        grid=(b, seq // tm),
        in_specs=[
            pl.BlockSpec((None, tm, POOL_W), lambda bi, i: (bi, i, 0)),
            pl.BlockSpec((None, hr, POOL_W), lambda bi, i: (bi, jnp.maximum(i * (tm // hr) - 1, 0), 0)),
            pl.BlockSpec((None, tm, A_W), lambda bi, i: (bi, i, 0)),
            pl.BlockSpec((None, tm, D_MODEL), lambda bi, i: (bi, i, 0)),
        ] + _mix_weight_specs(),
        out_specs=pl.BlockSpec((None, tm, D_MODEL), lambda bi, i: (bi, i, 0)),
        out_shape=jax.ShapeDtypeStruct((b, seq, D_MODEL), F32),
        compiler_params=_cparams(("parallel", "parallel")),
        name="mix_prompt",
    )(z3, z3, attn, x3, pw_bd, ps, woa, wop)


def _mix_step_kernel(sp_ref, u_ref, a_ref, x_ref, pw_ref, ps_ref, woa_ref, wop_ref, o_ref):
    u = u_ref[...]
    means, hist, j = [], u, POOL_HIST
    for w in POOL_WINDOWS:
        while j > POOL_HIST - (w - 1):
            j -= 1
            hist = hist + sp_ref[j]
        means.append(hist / float(w))
    pooled = _pick_by_group(means) - u
    o_ref[...] = _mix_tail(pooled, a_ref[...], x_ref[...], pw_ref, ps_ref, woa_ref, wop_ref)


def _mix_step(sp_t, us, attn, xs, pw_bd, ps, woa, wop):
    b = xs.shape[0]
    full = lambda shape: pl.BlockSpec(shape, lambda i: (0,) * len(shape))
    return pl.pallas_call(
        _mix_step_kernel,
        grid=(1,),
        in_specs=[full(sp_t.shape), full(us.shape), full(attn.shape), full(xs.shape)] + _mix_weight_specs(),
        out_specs=full((b, D_MODEL)),
        out_shape=jax.ShapeDtypeStruct((b, D_MODEL), F32),
        compiler_params=_cparams(("arbitrary",)),
        name="mix_step",
    )(sp_t, us, attn, xs, pw_bd, ps, woa, wop)


def _first_argmax(vals, lane):
    mx = jnp.max(vals, -1, keepdims=True)
    idx = jnp.min(jnp.where(vals == mx, lane, LANES), -1, keepdims=True)
    return mx, idx


def _cross_kernel(x_ref, g_ref, wq_ref, cqg_ref, mk_ref, mv_ref, wo_ref, gf_ref, wr_ref, br_ref,
                  y_ref, h_ref, rt_ref, hist_ref):
    x = x_ref[...]
    tm = x.shape[0]
    hq = _rms_rows(x, g_ref[...]).astype(BF16)
    q = jnp.dot(hq, wq_ref[...], preferred_element_type=F32)
    outs = []
    for h in range(X_HEADS):
        sl = slice(h * X_HD, (h + 1) * X_HD)
        qh = _rms_rows(q[:, sl], cqg_ref[...]).astype(BF16)
        kh = mk_ref[:, sl].astype(BF16)
        vh = mv_ref[:, sl].astype(BF16)
        s = lax.dot_general(qh, kh, (((1,), (1,)), ((), ())), preferred_element_type=F32) * X_SCALE
        p = jnp.exp(s - jnp.max(s, -1, keepdims=True))
        p = p / jnp.sum(p, -1, keepdims=True)
        outs.append(jnp.dot(p.astype(BF16), vh, preferred_element_type=F32))
    o = jnp.concatenate(outs, -1).astype(BF16)
    y = x + jnp.dot(o, wo_ref[...], preferred_element_type=F32)
    y_ref[...] = y

    hf = _rms_rows(y, gf_ref[...])
    h_bf = hf.astype(BF16)
    bits = lax.bitcast_convert_type(h_bf.astype(F32), U32)
    h_ref[...] = bits[:, :HALF] | (bits[:, HALF:] >> 16)

    lg = jnp.dot(h_bf, wr_ref[...], preferred_element_type=F32) + br_ref[...]

    lane = lax.broadcasted_iota(I32, lg.shape, 1)
    lgg = jnp.where(lane < N_GROUPS, lg, NEG)
    mg, g_sel = _first_argmax(lgg, lane)
    p_grp = 1.0 / jnp.sum(jnp.exp(lgg - mg), -1, keepdims=True)
    e_lo = N_GROUPS + E_PER_GROUP * g_sel
    le = jnp.where(jnp.logical_and(lane >= e_lo, lane < e_lo + E_PER_GROUP), lg, NEG)
    v0, i0 = _first_argmax(le, lane)
    v1, i1 = _first_argmax(jnp.where(lane == i0, NEG, le), lane)
    e1w = jnp.exp(v1 - v0)
    g0 = p_grp / (1.0 + e1w)
    g1 = p_grp * e1w / (1.0 + e1w)
    e0 = i0 - N_GROUPS
    e1 = i1 - N_GROUPS

    cnt = jnp.logical_or(lane == e0, lane == e1).astype(F32)
    if tm > 1:
        ri = lax.broadcasted_iota(I32, (tm, tm), 0)
        ci = lax.broadcasted_iota(I32, (tm, tm), 1)
        before = (ci < ri).astype(BF16)
        pre = jnp.dot(before, cnt.astype(BF16), preferred_element_type=F32)
        r0 = jnp.sum(jnp.where(lane == e0, pre, 0.0), -1, keepdims=True)
        r1 = jnp.sum(jnp.where(lane == e1, pre, 0.0), -1, keepdims=True)
    else:
        r0 = r1 = jnp.zeros_like(g0)
    hist_ref[...] = jnp.sum(cnt, axis=0, keepdims=True)

    rec = jnp.zeros(lg.shape, F32)
    for ln, val in ((RT_E0, e0.astype(F32)), (RT_E1, e1.astype(F32)), (RT_G0, g0), (RT_G1, g1),
                    (RT_R0, r0), (RT_R1, r1)):
        rec = jnp.where(lane == ln, val, rec)
    rt_ref[...] = rec


def _cross(x3, mk, mv, g, wq, cqg, wo, gf, wr, brt, *, tm):
    b, seq, _ = x3.shape
    nt = seq // tm
    zero = lambda bi, i: (0, 0)
    tok = lambda w: pl.BlockSpec((None, tm, w), lambda bi, i: (bi, i, 0))
    mem = pl.BlockSpec((None, N_MEM, X_W), lambda bi, i: (bi, 0, 0))
    return pl.pallas_call(
        _cross_kernel,
        grid=(b, nt),
        in_specs=[
            tok(D_MODEL),
            pl.BlockSpec((1, D_MODEL), zero),
            pl.BlockSpec((D_MODEL, X_W), zero),
            pl.BlockSpec((1, X_HD), zero),
            mem, mem,
            pl.BlockSpec((X_W, D_MODEL), zero),
            pl.BlockSpec((1, D_MODEL), zero),
            pl.BlockSpec((D_MODEL, LANES), zero),
            pl.BlockSpec((1, LANES), zero),
        ],
        out_specs=[tok(D_MODEL), tok(HALF), tok(LANES),
                   pl.BlockSpec((None, None, 1, LANES), lambda bi, i: (bi, i, 0, 0))],
        out_shape=[
            jax.ShapeDtypeStruct((b, seq, D_MODEL), F32),
            jax.ShapeDtypeStruct((b, seq, HALF), U32),
            jax.ShapeDtypeStruct((b, seq, LANES), F32),
            jax.ShapeDtypeStruct((b, nt, 1, LANES), F32),
        ],
        compiler_params=_cparams(("parallel", "parallel")),
        name="cross_b%d" % b,
    )(x3, g, wq, cqg, mk, mv, wo, gf, wr, brt)


def _pad_rows(n_tok, blk):
    return -(-(n_tok + blk) // blk) * blk


def _invmap_kernel(dest_ref, pend_ref, src_ref, *, n_assign, n_tok, n_blocks, blk):
    npad = _pad_rows(n_tok, blk)

    def fill(base):
        first = (lax.div(base, blk) & 1) * npad + n_tok

        def body(r, c):
            src_ref[base + r] = first + r
            return c
        lax.fori_loop(0, blk, body, 0, unroll=8)

    for e in range(N_EXPERTS):
        fill(jnp.maximum(pend_ref[e] - blk, 0))

    def fill_block(j, c):
        fill(j * blk)
        return c

    lax.fori_loop(pend_ref[N_EXPERTS - 1] // blk, n_blocks + 1, fill_block, 0)

    def put(a, c):
        src_ref[dest_ref[a]] = (a & 1) * npad + (a >> 1)
        return c

    lax.fori_loop(0, n_assign, put, 0, unroll=8)


def _invmap(dest, pend, *, n_tok, n_blocks, blk):
    assert n_blocks % 2 == 1, "the spare block must have odd parity"
    smem = pl.BlockSpec(memory_space=pltpu.SMEM)
    kern = functools.partial(_invmap_kernel, n_assign=dest.shape[0], n_tok=n_tok, n_blocks=n_blocks, blk=blk)
    return pl.pallas_call(
        kern,
        in_specs=[smem, smem],
        out_specs=smem,
        out_shape=jax.ShapeDtypeStruct(((n_blocks + 1) * blk,), I32),
        name="moe_invmap",
    )(dest, pend)


def _expert_kernel(be_ref, nb_ref, src_ref, h_ref, wg_ref, wu_ref, wd_ref, y2_ref,
                   xbuf0, xbuf1, obuf0, obuf1, wg_bf, wu_bf, wd_bf, gsem, ssem, *, blk, n_tok):
    i = pl.program_id(0)
    nb = nb_ref[0]
    n_rows = pl.num_programs(0) * blk
    npad = _pad_rows(n_tok, blk)
    xbufs, obufs = (xbuf0, xbuf1), (obuf0, obuf1)

    def gather_copy(tok, r, slot):
        return pltpu.make_async_copy(h_ref.at[pl.ds(tok, 1)], xbufs[slot].at[pl.ds(r, 1)], gsem.at[slot])

    def scatter_copy(r, dst, slot):
        return pltpu.make_async_copy(obufs[slot].at[pl.ds(r, 1)], y2_ref.at[pl.ds(dst, 1)], ssem.at[slot])

    def gather_start(base, slot):
        for r in range(blk):
            row = src_ref[base + r]
            gather_copy(jnp.where(row >= npad, row - npad, row), r, slot).start()

    def gather_wait(slot):
        for r in range(blk):
            gather_copy(0, r, slot).wait()

    def scatter_start(base, slot):
        for r in range(blk):
            scatter_copy(r, src_ref[base + r], slot).start()

    def scatter_wait(slot):
        for r in range(blk):
            scatter_copy(r, 0, slot).wait()

    @pl.when(i == 0)
    def _():
        obuf1[...] = jnp.zeros(obuf1.shape, F32)
        tail = [half * npad + n_tok + r for half in range(TOP_K) for r in range(npad - n_tok)]
        for t, dst in enumerate(tail):
            scatter_copy(t % blk, dst, 1).start()
        for t in range(len(tail)):
            scatter_copy(t % blk, 0, 1).wait()
        gather_start(0, 0)

    def step(slot):
        other = 1 - slot
        gather_wait(slot)
        changed = jnp.logical_or(i == 0, be_ref[i] != be_ref[jnp.maximum(i - 1, 0)])

        @pl.when(changed)
        def _():
            wg_bf[...] = wg_ref[...].astype(BF16)
            wu_bf[...] = wu_ref[...].astype(BF16)
            wd_bf[...] = wd_ref[...].astype(BF16)

        gather_start(jnp.minimum(i + 1, nb - 1) * blk, other)
        scatter_start(jnp.where(i > 0, (i - 1) * blk, n_rows), other)
        xu = xbufs[slot][...]
        xa = lax.bitcast_convert_type(xu & jnp.uint32(0xFFFF0000), F32).astype(BF16)
        xb = lax.bitcast_convert_type(xu << 16, F32).astype(BF16)
        x = jnp.concatenate([xa, xb], -1)
        a = jax.nn.silu(jnp.dot(x, wg_bf[...], preferred_element_type=F32))
        b = jnp.dot(x, wu_bf[...], preferred_element_type=F32)
        obufs[slot][...] = jnp.dot((a * b).astype(BF16), wd_bf[...], preferred_element_type=F32)
        scatter_wait(other)

        @pl.when(i == nb - 1)
        def _():
            scatter_start(i * blk, slot)
            scatter_wait(slot)
            gather_wait(other)

    @pl.when(jnp.logical_and(i < nb, i % 2 == 0))
    def _():
        step(0)

    @pl.when(jnp.logical_and(i < nb, i % 2 == 1))
    def _():
        step(1)


def _experts(blk_e, n_used, src, h_all, w_gate, w_up, w_down, *, blk):
    n_tok = h_all.shape[0] - blk
    wspec = lambda k, n: pl.BlockSpec((None, k, n), lambda i, be, nb, src: (be[i], 0, 0))
    return pl.pallas_call(
        functools.partial(_expert_kernel, blk=blk, n_tok=n_tok),
        grid_spec=pltpu.PrefetchScalarGridSpec(
            num_scalar_prefetch=3,
            grid=(blk_e.shape[0],),
            in_specs=[pl.BlockSpec(memory_space=pl.ANY),
                      wspec(D_MODEL, D_EXPERT), wspec(D_MODEL, D_EXPERT), wspec(D_EXPERT, D_MODEL)],
            out_specs=pl.BlockSpec(memory_space=pl.ANY),
            scratch_shapes=[
                pltpu.VMEM((blk, HALF), U32), pltpu.VMEM((blk, HALF), U32),
                pltpu.VMEM((blk, D_MODEL), F32), pltpu.VMEM((blk, D_MODEL), F32),
                pltpu.VMEM((D_MODEL, D_EXPERT), BF16),
                pltpu.VMEM((D_MODEL, D_EXPERT), BF16),
                pltpu.VMEM((D_EXPERT, D_MODEL), BF16),
                pltpu.SemaphoreType.DMA((2,)), pltpu.SemaphoreType.DMA((2,)),
            ],
        ),
        out_shape=jax.ShapeDtypeStruct((TOP_K * _pad_rows(n_tok, blk), D_MODEL), F32),
        compiler_params=_cparams(("arbitrary",)),
        name="experts",
    )(blk_e, n_used, src, h_all, w_gate, w_up, w_down)


def _combine_kernel(x_ref, rt_ref, y0_ref, y1_ref, o_ref):
    rt = rt_ref[...]
    g0 = rt[:, RT_G0:RT_G0 + 1]
    g1 = rt[:, RT_G1:RT_G1 + 1]
    o_ref[...] = x_ref[...] + (g0 * y0_ref[...] + g1 * y1_ref[...])


def _combine(x, rt, y2, *, tm, row0):
    n = x.shape[0]
    npad = y2.shape[0] // TOP_K
    off, half = row0 // tm, npad // tm
    assert off * tm == row0 and half * tm == npad
    return pl.pallas_call(
        _combine_kernel,
        grid=(n // tm,),
        in_specs=[pl.BlockSpec((tm, D_MODEL), lambda i: (i, 0)),
                  pl.BlockSpec((tm, LANES), lambda i: (i, 0)),
                  pl.BlockSpec((tm, D_MODEL), lambda i: (i + off, 0)),
                  pl.BlockSpec((tm, D_MODEL), lambda i: (i + off + half, 0))],
        out_specs=pl.BlockSpec((tm, D_MODEL), lambda i: (i, 0)),
        out_shape=jax.ShapeDtypeStruct((n, D_MODEL), F32),
        compiler_params=_cparams(("parallel",)),
        name="combine_n%d" % n,
    )(x, rt, y2, y2)


def _dispatch_plan(rt_tiles, hist_tiles, tile_sizes, *, blk):
    hist = jnp.concatenate(hist_tiles, axis=0)[:, :N_EXPERTS].astype(I32)
    counts = jnp.sum(hist, axis=0)
    padded = (counts + blk - 1) // blk * blk
    pend = jnp.cumsum(padded)
    pstart = pend - padded
    tile_base = pstart[None, :] + jnp.cumsum(hist, axis=0) - hist
    ids = jnp.arange(N_EXPERTS, dtype=I32)
    dests, t0 = [], 0
    for rt, ht, ts in zip(rt_tiles, hist_tiles, tile_sizes):
        nt = ht.shape[0]
        tb = jnp.repeat(tile_base[t0:t0 + nt], ts, axis=0)
        t0 += nt
        e = rt[:, RT_E0:RT_E1 + 1].astype(I32)
        r = rt[:, RT_R0:RT_R1 + 1].astype(I32)
        base = jnp.sum(jnp.where(e[:, :, None] == ids[None, None, :], tb[:, None, :], 0), axis=-1)
        dests.append((base + r).reshape(-1))
    return dests, pend


def kernel(x_prompt, x_sample, cache_win_k, cache_win_v, state_pool, cache_mem_k, cache_mem_v,
           mem_prompt, rel_bias, norm_mix_g, w_in, q_norm_g, k_norm_g, pool_w, pool_scale, w_out,
           norm_cross_g, norm_mem_g, w_cq, w_ck, w_cv, cq_norm_g, ck_norm_g, w_co,
           norm_ffn_g, w_rg, b_rg, w_re, b_re, w_e_gate, w_e_up, w_e_down):
    assert norm_mix_g.shape[0] == 1, "single-layer step"
    bp, seq, d = x_prompt.shape
    bs = x_sample.shape[0]
    wb = cache_win_k.shape[2]
    np_tok = bp * seq

    w_in_l = w_in[0]
    w_in_bf = jnp.concatenate([w_in_l[:, 3 * A_W:], w_in_l[:, :3 * A_W]], axis=1).astype(BF16)
    cg_in = jnp.concatenate([jnp.ones((POOL_W,), F32), jnp.tile(q_norm_g[0], A_HEADS) * A_SCALE,
                             jnp.tile(k_norm_g[0], A_HEADS), jnp.ones((A_W,), F32)])[None, :]
    g_mix = norm_mix_g[0][None, :]
    ng = len(POOL_WINDOWS)
    pw_bd = jnp.concatenate(
        [jnp.pad(pool_w[0, g], ((0, 0), (g * POOL_C, (ng - 1 - g) * POOL_C))) for g in range(ng)],
        axis=0).astype(BF16)
    ps = pool_scale[0][None, :]
    woa = w_out[0, :A_W].astype(BF16)
    wop = w_out[0, A_W:].astype(BF16)
    w_mem_bf = jnp.concatenate([w_ck[0], w_cv[0]], axis=1).astype(BF16)
    cg_mem = jnp.concatenate([jnp.tile(ck_norm_g[0], X_HEADS), jnp.ones((X_W,), F32)])[None, :]
    wq_bf = w_cq[0].astype(BF16)
    wo_bf = w_co[0].astype(BF16)
    wr_bf = jnp.pad(jnp.concatenate([w_rg[0], w_re[0]], axis=1),
                    ((0, 0), (0, LANES - N_GROUPS - N_EXPERTS))).astype(BF16)
    br = jnp.pad(jnp.concatenate([b_rg[0], b_re[0]]), (0, LANES - N_GROUPS - N_EXPERTS))[None, :]
    bias_tab = _prompt_bias_table(rel_bias)
    bias_s, bias0 = _step_bias_table(rel_bias, wb)

    proj_in = functools.partial(_proj, g=g_mix, w_bf=w_in_bf, colgain=cg_in,
                                norm_lo=Q_OFF // MXU_N, norm_hi=V_OFF // MXU_N, group=A_HD)

    zp = proj_in(x_prompt.reshape(np_tok, d), tm=1024, name="proj_in_prompt")
    z3 = zp.reshape(bp, seq, IN_W)
    attn_p = _attn_prompt(z3, bias_tab)
    xp1 = _mix_prompt(z3, attn_p, x_prompt, pw_bd, ps, woa, wop, tm=256)

    xs2 = x_sample.reshape(bs, d)
    zs = proj_in(xs2, tm=bs, name="proj_in_step")
    col = lambda off: zs[:, off:off + A_W].reshape(bs, A_HEADS, A_HD, 1)
    us = zs[:, :POOL_W]
    ck_t = jnp.transpose(cache_win_k[0], (0, 2, 3, 1))
    cv_t = jnp.transpose(cache_win_v[0], (0, 2, 3, 1))
    attn_s, nk_t, nv_t = _step_cache(col(Q_OFF), col(K_OFF), col(V_OFF), ck_t, cv_t, bias_s, bias0, hb=4)
    sp_t = jnp.transpose(state_pool[0], (1, 0, 2))
    xs1 = _mix_step(sp_t, us, attn_s.reshape(bs, A_W).astype(BF16), xs2, pw_bd, ps, woa, wop)

    zm = _proj(mem_prompt.reshape(bp * N_MEM, d), norm_mem_g[0][None, :], w_mem_bf, cg_mem,
               tm=512, norm_lo=0, norm_hi=X_W // MXU_N, group=X_HD, name="mem_kv")
    mk_p = zm[:, :X_W].reshape(bp, N_MEM, X_W)
    mv_p = zm[:, X_W:].reshape(bp, N_MEM, X_W)

    cross = functools.partial(_cross, g=norm_cross_g[0][None, :], wq=wq_bf, cqg=cq_norm_g[0][None, :],
                              wo=wo_bf, gf=norm_ffn_g[0][None, :], wr=wr_bf, brt=br)
    tm_x = 512
    xp2, hp, rtp, histp = cross(xp1, mk_p, mv_p, tm=tm_x)
    xs2b, hs, rts, hists = cross(xs1.reshape(bs, 1, d), cache_mem_k[0].reshape(bs, N_MEM, X_W),
                                 cache_mem_v[0].reshape(bs, N_MEM, X_W), tm=1)

    blk = 256
    n_all = np_tok + bs
    n_blocks = -(-(n_all * TOP_K + N_EXPERTS * (blk - 1)) // blk) | 1
    rtp2, rts2 = rtp.reshape(np_tok, LANES), rts.reshape(bs, LANES)
    dests, pend = _dispatch_plan(
        [rtp2, rts2], [histp.reshape(-1, LANES), hists.reshape(-1, LANES)], [tm_x, 1], blk=blk)
    starts = jnp.arange(n_blocks, dtype=I32) * blk
    blk_e = jnp.minimum(jnp.sum((pend[None, :] <= starts[:, None]).astype(I32), axis=1), N_EXPERTS - 1)
    n_used = (pend[-1:] // blk).astype(I32)
    src = _invmap(jnp.concatenate(dests), pend, n_tok=n_all, n_blocks=n_blocks, blk=blk)
    h_all = jnp.concatenate([hp.reshape(np_tok, HALF), hs.reshape(bs, HALF), jnp.zeros((blk, HALF), U32)])
    y2 = _experts(blk_e, n_used, src, h_all, w_e_gate[0], w_e_up[0], w_e_down[0], blk=blk)
    y_prompt = _combine(xp2.reshape(np_tok, d), rtp2, y2, tm=256, row0=0).reshape(bp, seq, d)
    y_sample = _combine(xs2b.reshape(bs, d), rts2, y2, tm=bs, row0=np_tok).reshape(bs, 1, d)

    win_k_p = z3[:, seq - min(MAX_DISTANCE, seq):, K_OFF:V_OFF].reshape(1, bp, -1, A_HEADS, A_HD)
    win_v_p = z3[:, seq - min(MAX_DISTANCE, seq):, V_OFF:].reshape(1, bp, -1, A_HEADS, A_HD)
    win_k_s = jnp.transpose(nk_t, (0, 3, 1, 2))[None]
    win_v_s = jnp.transpose(nv_t, (0, 3, 1, 2))[None]
    pool_p = z3[:, seq - POOL_HIST:, :POOL_W][None]
    pool_s = jnp.transpose(jnp.concatenate([sp_t[1:], us[None]], axis=0), (1, 0, 2))[None]
    mem_k_p = mk_p.reshape(1, bp, N_MEM, X_HEADS, X_HD)
    mem_v_p = mv_p.reshape(1, bp, N_MEM, X_HEADS, X_HD)
    return (y_prompt, y_sample, win_k_p, win_v_p, win_k_s, win_v_s, pool_p, pool_s, mem_k_p, mem_v_p)
```

```python
import functools

import numpy as np
import jax
import jax.numpy as jnp
from jax import lax
from jax.experimental import pallas as pl
from jax.experimental.pallas import tpu as pltpu

F32 = jnp.float32
BF16 = jnp.bfloat16
U32 = jnp.uint32
I32 = jnp.int32

D_MODEL = 2048
A_HEADS = 12
A_HD = 64
A_W = A_HEADS * A_HD
BRANCHES = ((128, 1), (512, 4), (2048, 16))
NK = 128
NUM_BUCKETS = 32
MAX_DISTANCE = 2048
POOL_WINDOWS = (2, 4, 8, 16)
POOL_W = D_MODEL - A_W
POOL_C = POOL_W // len(POOL_WINDOWS)
POOL_HIST = 15
IN_W = 3 * A_W + POOL_W
N_MEM = 256
X_HEADS = 4
X_HD = 128
X_W = X_HEADS * X_HD
X_SCALE = X_HD ** -0.5
A_SCALE = A_HD ** -0.5
N_GROUPS = 4
E_PER_GROUP = 8
N_EXPERTS = N_GROUPS * E_PER_GROUP
TOP_K = 2
D_EXPERT = 512
EPS = 1e-6
NEG = -1e30

LANES = 128
MXU_N = 256
VMEM_LIMIT = 52 * 1024 * 1024
HALF = D_MODEL // 2

U_OFF, Q_OFF, K_OFF, V_OFF = 0, POOL_W, POOL_W + A_W, POOL_W + 2 * A_W

RT_E0, RT_E1, RT_G0, RT_G1, RT_R0, RT_R1 = range(6)


def _cparams(sem):
    return pltpu.CompilerParams(dimension_semantics=sem, vmem_limit_bytes=VMEM_LIMIT)


def _rms_rows(x, g):
    ms = jnp.mean(x * x, axis=-1, keepdims=True)
    return x * lax.rsqrt(ms + EPS) * g


def _pack_pairs(x):
    bits = lax.bitcast_convert_type(x.astype(BF16).astype(F32), U32)
    half = x.shape[1] // 2
    return bits[:, :half] | (bits[:, half:] >> 16)


def _unpack_pairs(u):
    return (lax.bitcast_convert_type(u & jnp.uint32(0xFFFF0000), F32), lax.bitcast_convert_type(u << 16, F32))


def _t5_bucket(dist):
    n = np.asarray(dist, dtype=np.int64)
    exact = NUM_BUCKETS // 2
    ratio = np.log(np.maximum(n, 1) / exact) / np.log(MAX_DISTANCE / exact)
    large = np.minimum(exact + (ratio * (NUM_BUCKETS - exact)).astype(np.int64), NUM_BUCKETS - 1)
    return np.where(n < exact, n, large).astype(np.int32)


def _bias_lookup(rel_bias, buckets, ok):
    buckets = np.asarray(buckets, np.int32)
    onehot = (jnp.arange(NUM_BUCKETS, dtype=I32)[:, None] == jnp.asarray(buckets.reshape(1, -1))).astype(F32)
    vals = jnp.einsum("bh,bn->hn", rel_bias.astype(F32), onehot, precision=lax.Precision.HIGHEST)
    vals = jnp.where(jnp.asarray(np.asarray(ok).reshape(1, -1)), vals, NEG)
    return vals.reshape((A_HEADS,) + buckets.shape)


def _proj_kernel(x_ref, g_ref, w_ref, cg_ref, o_ref, h_scr, *, norm_lo, norm_hi, group):
    j = pl.program_id(1)

    @pl.when(j == 0)
    def _():
        h_scr[...] = _rms_rows(x_ref[...], g_ref[...]).astype(BF16)

    z = jnp.dot(h_scr[...], w_ref[...], preferred_element_type=F32)
    is_norm = jnp.logical_and(j >= norm_lo, j < norm_hi)

    @pl.when(is_norm)
    def _():
        lo_mask = lax.broadcasted_iota(I32, (1, LANES), 1) < group
        outs = []
        for c in range(z.shape[1] // LANES):
            zc = z[:, c * LANES:(c + 1) * LANES]
            zz = zc * zc
            if group == LANES:
                r = lax.rsqrt(jnp.sum(zz, -1, keepdims=True) * (1.0 / group) + EPS)
            else:
                lo = jnp.sum(jnp.where(lo_mask, zz, 0.0), -1, keepdims=True) * (1.0 / group)
                hi = jnp.sum(jnp.where(lo_mask, 0.0, zz), -1, keepdims=True) * (1.0 / group)
                r = jnp.where(lo_mask, lax.rsqrt(lo + EPS), lax.rsqrt(hi + EPS))
            outs.append(zc * r)
        o_ref[...] = jnp.concatenate(outs, -1) * cg_ref[...]

    @pl.when(jnp.logical_not(is_norm))
    def _():
        o_ref[...] = z


def _proj(x, g, w_bf, colgain, *, tm, norm_lo, norm_hi, group, name):
    n, d = x.shape
    width = w_bf.shape[1]
    kern = functools.partial(_proj_kernel, norm_lo=norm_lo, norm_hi=norm_hi, group=group)
    return pl.pallas_call(
        kern,
        grid=(n // tm, width // MXU_N),
        in_specs=[
            pl.BlockSpec((tm, d), lambda i, j: (i, 0)),
            pl.BlockSpec((1, d), lambda i, j: (0, 0)),
            pl.BlockSpec((d, MXU_N), lambda i, j: (0, j)),
            pl.BlockSpec((1, MXU_N), lambda i, j: (0, j)),
        ],
        out_specs=pl.BlockSpec((tm, MXU_N), lambda i, j: (i, j)),
        out_shape=jax.ShapeDtypeStruct((n, width), F32),
        scratch_shapes=[pltpu.VMEM((tm, d), BF16)],
        compiler_params=_cparams(("parallel", "arbitrary")),
        name=name,
    )(x, g, w_bf, colgain)


def _attn_prompt_kernel(q_ref, k_ref, v_ref, bias_ref, o_ref, m_scr, l_scr, acc_scr, *, seq):
    lo = lax.broadcasted_iota(I32, (NK, LANES), 1) < A_HD

    for br, (_, d) in enumerate(BRANCHES):
        nblk = seq // (d * NK)

        def rows(start, d=d):
            return pl.ds(start, NK, stride=d) if d > 1 else pl.ds(start, NK)

        for r in range(d):
            for qb in range(nblk):
                cur = rows(r + d * NK * qb)
                dst = pl.ds((r * nblk + qb) * NK, NK)
                qf = q_ref[cur, :]
                if qb > 0:
                    prev = rows(r + d * NK * (qb - 1))
                    kk = jnp.concatenate([k_ref[prev, :], k_ref[cur, :]], 0)
                    vv = jnp.concatenate([v_ref[prev, :], v_ref[cur, :]], 0)
                    off = 0
                else:
                    kk, vv, off = k_ref[cur, :], v_ref[cur, :], NK
                kk = kk.astype(BF16)
                vv = vv.astype(BF16)
                ms, ls, pvs = [], [], []
                for h in range(2):
                    qh = jnp.where(lo if h == 0 else jnp.logical_not(lo), qf, 0.0).astype(BF16)
                    s = lax.dot_general(qh, kk, (((1,), (1,)), ((), ())), preferred_element_type=F32)
                    s = s + bias_ref[br, h, :, off:2 * NK]
                    m = jnp.max(s, -1, keepdims=True)
                    p = jnp.exp(s - m)
                    ms.append(m)
                    ls.append(jnp.sum(p, -1, keepdims=True))
                    pvs.append(jnp.dot(p.astype(BF16), vv, preferred_element_type=F32))
                acc_scr[br, dst, :] = jnp.where(lo, pvs[0], pvs[1])
                m_scr[br, dst, :] = jnp.where(lo, ms[0], ms[1])
                l_scr[br, dst, :] = jnp.where(lo, ls[0], ls[1])

    d4, d16 = BRANCHES[1][1], BRANCHES[2][1]
    for r in range(d16):
        for qb in range(seq // (d16 * NK)):
            nat = pl.ds(r + d16 * NK * qb, NK, stride=d16)
            views = (nat,
                     pl.ds((r % d4) * (seq // d4) + r // d4 + (d16 // d4) * NK * qb, NK, stride=d16 // d4),
                     pl.ds(r * (seq // d16) + NK * qb, NK))
            m = [m_scr[i, v, :] for i, v in enumerate(views)]
            mx = jnp.maximum(jnp.maximum(m[0], m[1]), m[2])
            w = [jnp.exp(mi - mx) for mi in m]
            num = sum(wi * acc_scr[i, v, :] for i, (wi, v) in enumerate(zip(w, views)))
            den = sum(wi * l_scr[i, v, :] for i, (wi, v) in enumerate(zip(w, views)))
            o_ref[nat, :] = num / den


def _attn_prompt(z3, bias_tab):
    b, seq, _ = z3.shape
    assert seq % BRANCHES[-1][0] == 0, "sequence must be a multiple of the widest window"
    nhp = A_W // LANES
    nbr = len(BRANCHES)
    qo, ko, vo = Q_OFF // LANES, K_OFF // LANES, V_OFF // LANES
    kern = functools.partial(_attn_prompt_kernel, seq=seq)
    return pl.pallas_call(
        kern,
        grid=(b, nhp),
        in_specs=[
            pl.BlockSpec((None, seq, LANES), lambda i, hp: (i, 0, qo + hp)),
            pl.BlockSpec((None, seq, LANES), lambda i, hp: (i, 0, ko + hp)),
            pl.BlockSpec((None, seq, LANES), lambda i, hp: (i, 0, vo + hp)),
            pl.BlockSpec((len(BRANCHES), None, 2, NK, 2 * NK), lambda i, hp: (0, hp, 0, 0, 0)),
        ],
        out_specs=pl.BlockSpec((None, seq, LANES), lambda i, hp: (i, 0, hp)),
        out_shape=jax.ShapeDtypeStruct((b, seq, A_W), F32),
        scratch_shapes=[pltpu.VMEM((nbr, seq, LANES), F32)] * 3,
        compiler_params=_cparams(("parallel", "parallel")),
        name="attn_prompt",
    )(z3, z3, z3, bias_tab)


def _prompt_bias_table(rel_bias):
    qi = np.arange(NK)[:, None]
    kj = np.arange(2 * NK)[None, :]
    rel = qi + NK - kj
    ok = (rel >= 0) & (rel <= NK)
    bkt = np.stack([_t5_bucket(np.clip(rel, 0, None) * d) for _, d in BRANCHES])
    tab = _bias_lookup(rel_bias, bkt, np.broadcast_to(ok, bkt.shape))
    return jnp.transpose(tab, (1, 0, 2, 3)).reshape(len(BRANCHES), A_HEADS // 2, 2, NK, 2 * NK)


BR_ROWS = 8


def _step_cache_kernel(q_ref, kn_ref, vn_ref, ck_ref, cv_ref, bias_ref, b0_ref, o_ref, ok_ref, ov_ref,
                       *, hb, wb):
    last = lax.broadcasted_iota(I32, (1, wb), 1) == wb - 1
    rnd = lambda t: t.astype(BF16).astype(F32)
    for h in range(hb):
        q = rnd(q_ref[h])
        kt, vt = ck_ref[h], cv_ref[h]
        kn, vn = kn_ref[h], vn_ref[h]
        s = jnp.sum(rnd(kt) * q, axis=0, keepdims=True)
        sb = s + bias_ref[h]
        s_new = jnp.sum(rnd(kn) * q, axis=0, keepdims=True) + b0_ref[h]
        m = jnp.maximum(jnp.max(sb, axis=1, keepdims=True), s_new)
        p = jnp.exp(sb - m)
        p_new = jnp.exp(s_new - m)
        l = jnp.sum(p, axis=1, keepdims=True) + p_new
        is_branch = lax.broadcasted_iota(I32, (BR_ROWS, 1), 0) < len(BRANCHES)
        lse = jnp.where(is_branch, m + jnp.log(l), NEG)
        e = jnp.exp(lse - jnp.max(lse, axis=0, keepdims=True))
        wts = e / jnp.sum(e, axis=0, keepdims=True)
        pr, pnr, vtr, vnr = rnd(p), rnd(p_new), rnd(vt), rnd(vn)
        acc = jnp.zeros_like(vn)
        for br in range(len(BRANCHES)):
            rows = slice(br, br + 1)
            o_br = (jnp.sum(vtr * pr[rows], axis=1, keepdims=True) + pnr[rows] * vnr) / l[rows]
            acc = acc + wts[rows] * o_br
        o_ref[h] = acc
        ok_ref[h] = jnp.where(last, kn, pltpu.roll(kt, wb - 1, 1))
        ov_ref[h] = jnp.where(last, vn, pltpu.roll(vt, wb - 1, 1))


def _step_cache(qc, knc, vnc, ck_t, cv_t, bias_s, bias0, *, hb):
    b, _, _, wb = ck_t.shape
    col = pl.BlockSpec((None, hb, A_HD, 1), lambda i, j: (i, j, 0, 0))
    cache = pl.BlockSpec((None, hb, A_HD, wb), lambda i, j: (i, j, 0, 0))
    kern = functools.partial(_step_cache_kernel, hb=hb, wb=wb)
    return pl.pallas_call(
        kern,
        grid=(b, A_HEADS // hb),
        in_specs=[col, col, col, cache, cache,
                  pl.BlockSpec((hb, BR_ROWS, wb), lambda i, j: (j, 0, 0)),
                  pl.BlockSpec((hb, 1, 1), lambda i, j: (j, 0, 0))],
        out_specs=[col, cache, cache],
        out_shape=[jax.ShapeDtypeStruct((b, A_HEADS, A_HD, 1), F32),
                   jax.ShapeDtypeStruct(ck_t.shape, F32),
                   jax.ShapeDtypeStruct(cv_t.shape, F32)],
        compiler_params=_cparams(("parallel", "parallel")),
        name="step_cache",
    )(qc, knc, vnc, ck_t, cv_t, bias_s, bias0)


def _step_bias_table(rel_bias, wb):
    dist = wb - np.arange(wb)
    bkt = np.zeros((BR_ROWS, wb), np.int32)
    ok = np.zeros((BR_ROWS, wb), bool)
    for br, (w, d) in enumerate(BRANCHES):
        bkt[br] = _t5_bucket(dist)
        ok[br] = (dist <= w) & (dist % d == 0)
    tab = _bias_lookup(rel_bias, bkt, ok)
    bias0 = _bias_lookup(rel_bias, _t5_bucket(np.zeros((1, 1))), np.ones((1, 1), bool))
    return tab, bias0


def _mix_tail(pooled, attn, x, pw_ref, ps_ref, woa_ref, wop_ref):
    pb = jnp.dot(pooled.astype(BF16), pw_ref[...], preferred_element_type=F32) * ps_ref[...]
    y = jnp.dot(attn.astype(BF16), woa_ref[...], preferred_element_type=F32)
    y = y + jnp.dot(pb.astype(BF16), wop_ref[...], preferred_element_type=F32)
    return x + y


def _pick_by_group(vals):
    col = lax.broadcasted_iota(I32, (1, POOL_W), 1)
    out = vals[-1]
    for g in range(len(vals) - 2, -1, -1):
        out = jnp.where(col < (g + 1) * POOL_C, vals[g], out)
    return out


def _mix_prompt_kernel(u_ref, uh_ref, a_ref, x_ref, pw_ref, ps_ref, woa_ref, wop_ref, o_ref, *, tm):
    i = pl.program_id(1)
    u = u_ref[...]
    halo = jnp.where(i > 0, uh_ref[...], 0.0)
    e = jnp.concatenate([halo, u], 0)
    hr = halo.shape[0]
    sums, s, w = [], e, 1
    while w < POOL_WINDOWS[-1]:
        s = s + pltpu.roll(s, w, 0)
        w *= 2
        sums.append(s)
    pos = i * tm + lax.broadcasted_iota(I32, (tm, 1), 0)
    means = [sw[hr:] / jnp.minimum(wd, pos + 1).astype(F32) for sw, wd in zip(sums, POOL_WINDOWS)]
    pooled = _pick_by_group(means) - u
    o_ref[...] = _mix_tail(pooled, a_ref[...], x_ref[...], pw_ref, ps_ref, woa_ref, wop_ref)


def _mix_weight_specs():
    zero = lambda *_: (0, 0)
    return [
        pl.BlockSpec((POOL_W, POOL_W), zero),
        pl.BlockSpec((1, POOL_W), zero),
        pl.BlockSpec((A_W, D_MODEL), zero),
        pl.BlockSpec((POOL_W, D_MODEL), zero),
    ]


def _mix_prompt(z3, attn, x3, pw_bd, ps, woa, wop, *, tm):
    b, seq, _ = x3.shape
    hr = 16
    kern = functools.partial(_mix_prompt_kernel, tm=tm)
    return pl.pallas_call(
        kern,
        grid=(b, seq // tm),
        in_specs=[
            pl.BlockSpec((None, tm, POOL_W), lambda bi, i: (bi, i, 0)),
            pl.BlockSpec((None, hr, POOL_W), lambda bi, i: (bi, jnp.maximum(i * (tm // hr) - 1, 0), 0)),
            pl.BlockSpec((None, tm, A_W), lambda bi, i: (bi, i, 0)),
            pl.BlockSpec((None, tm, D_MODEL), lambda bi, i: (bi, i, 0)),
        ] + _mix_weight_specs(),
        out_specs=pl.BlockSpec((None, tm, D_MODEL), lambda bi, i: (bi, i, 0)),
        out_shape=jax.ShapeDtypeStruct((b, seq, D_MODEL), F32),
        compiler_params=_cparams(("parallel", "parallel")),
        name="mix_prompt",
    )(z3, z3, attn, x3, pw_bd, ps, woa, wop)


def _mix_step_kernel(sp_ref, u_ref, a_ref, x_ref, pw_ref, ps_ref, woa_ref, wop_ref, o_ref):
    u = u_ref[...]
    means, hist, j = [], u, POOL_HIST
    for w in POOL_WINDOWS:
        while j > POOL_HIST - (w - 1):
            j -= 1
            hist = hist + sp_ref[j]
        means.append(hist / float(w))
    pooled = _pick_by_group(means) - u
    o_ref[...] = _mix_tail(pooled, a_ref[...], x_ref[...], pw_ref, ps_ref, woa_ref, wop_ref)


def _mix_step(sp_t, us, attn, xs, pw_bd, ps, woa, wop):
    b = xs.shape[0]
    full = lambda shape: pl.BlockSpec(shape, lambda i: (0,) * len(shape))
    return pl.pallas_call(
        _mix_step_kernel,
        grid=(1,),
        in_specs=[full(sp_t.shape), full(us.shape), full(attn.shape), full(xs.shape)] + _mix_weight_specs(),
        out_specs=full((b, D_MODEL)),
        out_shape=jax.ShapeDtypeStruct((b, D_MODEL), F32),
        compiler_params=_cparams(("arbitrary",)),
        name="mix_step",
    )(sp_t, us, attn, xs, pw_bd, ps, woa, wop)


def _first_argmax(vals, lane):
    mx = jnp.max(vals, -1, keepdims=True)
    idx = jnp.min(jnp.where(vals == mx, lane, LANES), -1, keepdims=True)
    return mx, idx


def _cross_kernel(x_ref, g_ref, wq_ref, cqg_ref, mk_ref, mv_ref, wo_ref, gf_ref, wr_ref, br_ref,
                  y_ref, h_ref, rt_ref, hist_ref):
    x = x_ref[...]
    tm = x.shape[0]
    hq = _rms_rows(x, g_ref[...]).astype(BF16)
    q = jnp.dot(hq, wq_ref[...], preferred_element_type=F32)
    outs = []
    for h in range(X_HEADS):
        sl = slice(h * X_HD, (h + 1) * X_HD)
        qh = _rms_rows(q[:, sl], cqg_ref[...]).astype(BF16)
        kh = mk_ref[:, sl].astype(BF16)
        vh = mv_ref[:, sl].astype(BF16)
        s = lax.dot_general(qh, kh, (((1,), (1,)), ((), ())), preferred_element_type=F32) * X_SCALE
        p = jnp.exp(s - jnp.max(s, -1, keepdims=True))
        p = p / jnp.sum(p, -1, keepdims=True)
        outs.append(jnp.dot(p.astype(BF16), vh, preferred_element_type=F32))
    o = jnp.concatenate(outs, -1).astype(BF16)
    y = x + jnp.dot(o, wo_ref[...], preferred_element_type=F32)
    y_ref[...] = y

    hf = _rms_rows(y, gf_ref[...])
    h_bf = hf.astype(BF16)
    h_ref[...] = _pack_pairs(hf)

    lg = jnp.dot(h_bf, wr_ref[...], preferred_element_type=F32) + br_ref[...]

    lane = lax.broadcasted_iota(I32, lg.shape, 1)
    lgg = jnp.where(lane < N_GROUPS, lg, NEG)
    mg, g_sel = _first_argmax(lgg, lane)
    p_grp = 1.0 / jnp.sum(jnp.exp(lgg - mg), -1, keepdims=True)
    e_lo = N_GROUPS + E_PER_GROUP * g_sel
    le = jnp.where(jnp.logical_and(lane >= e_lo, lane < e_lo + E_PER_GROUP), lg, NEG)
    v0, i0 = _first_argmax(le, lane)
    v1, i1 = _first_argmax(jnp.where(lane == i0, NEG, le), lane)
    e1w = jnp.exp(v1 - v0)
    g0 = p_grp / (1.0 + e1w)
    g1 = p_grp * e1w / (1.0 + e1w)
    e0 = i0 - N_GROUPS
    e1 = i1 - N_GROUPS

    cnt = jnp.logical_or(lane == e0, lane == e1).astype(F32)
    if tm > 1:
        ri = lax.broadcasted_iota(I32, (tm, tm), 0)
        ci = lax.broadcasted_iota(I32, (tm, tm), 1)
        before = (ci < ri).astype(BF16)
        pre = jnp.dot(before, cnt.astype(BF16), preferred_element_type=F32)
        r0 = jnp.sum(jnp.where(lane == e0, pre, 0.0), -1, keepdims=True)
        r1 = jnp.sum(jnp.where(lane == e1, pre, 0.0), -1, keepdims=True)
    else:
        r0 = r1 = jnp.zeros_like(g0)
    hist_ref[...] = jnp.sum(cnt, axis=0, keepdims=True)

    rec = jnp.zeros(lg.shape, F32)
    for ln, val in ((RT_E0, e0.astype(F32)), (RT_E1, e1.astype(F32)), (RT_G0, g0), (RT_G1, g1),
                    (RT_R0, r0), (RT_R1, r1)):
        rec = jnp.where(lane == ln, val, rec)
    rt_ref[...] = rec


def _cross(x3, mk, mv, g, wq, cqg, wo, gf, wr, brt, *, tm):
    b, seq, _ = x3.shape
    nt = seq // tm
    zero = lambda bi, i: (0, 0)
    tok = lambda w: pl.BlockSpec((None, tm, w), lambda bi, i: (bi, i, 0))
    mem = pl.BlockSpec((None, N_MEM, X_W), lambda bi, i: (bi, 0, 0))
    return pl.pallas_call(
        _cross_kernel,
        grid=(b, nt),
        in_specs=[
            tok(D_MODEL),
            pl.BlockSpec((1, D_MODEL), zero),
            pl.BlockSpec((D_MODEL, X_W), zero),
            pl.BlockSpec((1, X_HD), zero),
            mem, mem,
            pl.BlockSpec((X_W, D_MODEL), zero),
            pl.BlockSpec((1, D_MODEL), zero),
            pl.BlockSpec((D_MODEL, LANES), zero),
            pl.BlockSpec((1, LANES), zero),
        ],
        out_specs=[tok(D_MODEL), tok(HALF), tok(LANES),
                   pl.BlockSpec((None, None, 1, LANES), lambda bi, i: (bi, i, 0, 0))],
        out_shape=[
            jax.ShapeDtypeStruct((b, seq, D_MODEL), F32),
            jax.ShapeDtypeStruct((b, seq, HALF), U32),
            jax.ShapeDtypeStruct((b, seq, LANES), F32),
            jax.ShapeDtypeStruct((b, nt, 1, LANES), F32),
        ],
        compiler_params=_cparams(("parallel", "parallel")),
        name="cross_b%d" % b,
    )(x3, g, wq, cqg, mk, mv, wo, gf, wr, brt)


def _pad_rows(n_tok, blk):
    return -(-(n_tok + blk) // blk) * blk


def _invmap_kernel(dest_ref, pend_ref, src_ref, *, n_assign, n_tok, n_blocks, blk):
    npad = _pad_rows(n_tok, blk)

    def fill(base):
        first = (lax.div(base, blk) & 1) * npad + n_tok

        def body(r, c):
            src_ref[base + r] = first + r
            return c
        lax.fori_loop(0, blk, body, 0, unroll=8)

    for e in range(N_EXPERTS):
        fill(jnp.maximum(pend_ref[e] - blk, 0))

    def fill_block(j, c):
        fill(j * blk)
        return c

    lax.fori_loop(pend_ref[N_EXPERTS - 1] // blk, n_blocks + 1, fill_block, 0)

    def put(a, c):
        src_ref[dest_ref[a]] = (a & 1) * npad + (a >> 1)
        return c

    lax.fori_loop(0, n_assign, put, 0, unroll=8)


def _invmap(dest, pend, *, n_tok, n_blocks, blk):
    assert n_blocks % 2 == 1, "the spare block must have odd parity"
    smem = pl.BlockSpec(memory_space=pltpu.SMEM)
    kern = functools.partial(_invmap_kernel, n_assign=dest.shape[0], n_tok=n_tok, n_blocks=n_blocks, blk=blk)
    return pl.pallas_call(
        kern,
        in_specs=[smem, smem],
        out_specs=smem,
        out_shape=jax.ShapeDtypeStruct(((n_blocks + 1) * blk,), I32),
        name="moe_invmap",
    )(dest, pend)


def _expert_kernel(be_ref, nb_ref, src_ref, h_ref, wg_ref, wu_ref, wd_ref, y2_ref,
                   xbuf0, xbuf1, obuf0, obuf1, wg_bf, wu_bf, wd_bf, gsem, ssem, *, blk, n_tok):
    i = pl.program_id(0)
    nb = nb_ref[0]
    n_rows = pl.num_programs(0) * blk
    npad = _pad_rows(n_tok, blk)
    xbufs, obufs = (xbuf0, xbuf1), (obuf0, obuf1)

    def gather_copy(tok, r, slot):
        return pltpu.make_async_copy(h_ref.at[pl.ds(tok, 1)], xbufs[slot].at[pl.ds(r, 1)], gsem.at[slot])

    def scatter_copy(r, dst, slot):
        return pltpu.make_async_copy(obufs[slot].at[pl.ds(r, 1)], y2_ref.at[pl.ds(dst, 1)], ssem.at[slot])

    def gather_start(base, slot):
        for r in range(blk):
            row = src_ref[base + r]
            gather_copy(jnp.where(row >= npad, row - npad, row), r, slot).start()

    def gather_wait(slot):
        for r in range(blk):
            gather_copy(0, r, slot).wait()

    def scatter_start(base, slot):
        for r in range(blk):
            scatter_copy(r, src_ref[base + r], slot).start()

    def scatter_wait(slot):
        for r in range(blk):
            scatter_copy(r, 0, slot).wait()

    @pl.when(i == 0)
    def _():
        obuf1[...] = jnp.zeros(obuf1.shape, U32)
        tail = [half * npad + n_tok + r for half in range(TOP_K) for r in range(npad - n_tok)]
        for t, dst in enumerate(tail):
            scatter_copy(t % blk, dst, 1).start()
        for t in range(len(tail)):
            scatter_copy(t % blk, 0, 1).wait()
        gather_start(0, 0)

    def step(slot):
        other = 1 - slot
        gather_wait(slot)
        changed = jnp.logical_or(i == 0, be_ref[i] != be_ref[jnp.maximum(i - 1, 0)])

        @pl.when(changed)
        def _():
            wg_bf[...] = wg_ref[...].astype(BF16)
            wu_bf[...] = wu_ref[...].astype(BF16)
            wd_bf[...] = wd_ref[...].astype(BF16)

        gather_start(jnp.minimum(i + 1, nb - 1) * blk, other)
        scatter_start(jnp.where(i > 0, (i - 1) * blk, n_rows), other)
        x = jnp.concatenate([t.astype(BF16) for t in _unpack_pairs(xbufs[slot][...])], -1)
        a = jax.nn.silu(jnp.dot(x, wg_bf[...], preferred_element_type=F32))
        b = jnp.dot(x, wu_bf[...], preferred_element_type=F32)
        obufs[slot][...] = _pack_pairs(jnp.dot((a * b).astype(BF16), wd_bf[...], preferred_element_type=F32))
        scatter_wait(other)

        @pl.when(i == nb - 1)
        def _():
            scatter_start(i * blk, slot)
            scatter_wait(slot)
            gather_wait(other)

    @pl.when(jnp.logical_and(i < nb, i % 2 == 0))
    def _():
        step(0)

    @pl.when(jnp.logical_and(i < nb, i % 2 == 1))
    def _():
        step(1)


def _experts(blk_e, n_used, src, h_all, w_gate, w_up, w_down, *, blk):
    n_tok = h_all.shape[0] - blk
    wspec = lambda k, n: pl.BlockSpec((None, k, n), lambda i, be, nb, src: (be[i], 0, 0))
    return pl.pallas_call(
        functools.partial(_expert_kernel, blk=blk, n_tok=n_tok),
        grid_spec=pltpu.PrefetchScalarGridSpec(
            num_scalar_prefetch=3,
            grid=(blk_e.shape[0],),
            in_specs=[pl.BlockSpec(memory_space=pl.ANY),
                      wspec(D_MODEL, D_EXPERT), wspec(D_MODEL, D_EXPERT), wspec(D_EXPERT, D_MODEL)],
            out_specs=pl.BlockSpec(memory_space=pl.ANY),
            scratch_shapes=[
                pltpu.VMEM((blk, HALF), U32), pltpu.VMEM((blk, HALF), U32),
                pltpu.VMEM((blk, HALF), U32), pltpu.VMEM((blk, HALF), U32),
                pltpu.VMEM((D_MODEL, D_EXPERT), BF16),
                pltpu.VMEM((D_MODEL, D_EXPERT), BF16),
                pltpu.VMEM((D_EXPERT, D_MODEL), BF16),
                pltpu.SemaphoreType.DMA((2,)), pltpu.SemaphoreType.DMA((2,)),
            ],
        ),
        out_shape=jax.ShapeDtypeStruct((TOP_K * _pad_rows(n_tok, blk), HALF), U32),
        compiler_params=_cparams(("arbitrary",)),
        name="experts",
    )(blk_e, n_used, src, h_all, w_gate, w_up, w_down)


def _combine_kernel(x_ref, rt_ref, y0_ref, y1_ref, o_ref):
    rt = rt_ref[...]
    g0 = rt[:, RT_G0:RT_G0 + 1]
    g1 = rt[:, RT_G1:RT_G1 + 1]
    y0, y1 = _unpack_pairs(y0_ref[...]), _unpack_pairs(y1_ref[...])
    o_ref[:, :HALF] = x_ref[:, :HALF] + (g0 * y0[0] + g1 * y1[0])
    o_ref[:, HALF:] = x_ref[:, HALF:] + (g0 * y0[1] + g1 * y1[1])


def _combine(x, rt, y2, *, tm, row0):
    n = x.shape[0]
    npad = y2.shape[0] // TOP_K
    off, half = row0 // tm, npad // tm
    assert off * tm == row0 and half * tm == npad
    return pl.pallas_call(
        _combine_kernel,
        grid=(n // tm,),
        in_specs=[pl.BlockSpec((tm, D_MODEL), lambda i: (i, 0)),
                  pl.BlockSpec((tm, LANES), lambda i: (i, 0)),
                  pl.BlockSpec((tm, HALF), lambda i: (i + off, 0)),
                  pl.BlockSpec((tm, HALF), lambda i: (i + off + half, 0))],
        out_specs=pl.BlockSpec((tm, D_MODEL), lambda i: (i, 0)),
        out_shape=jax.ShapeDtypeStruct((n, D_MODEL), F32),
        compiler_params=_cparams(("parallel",)),
        name="combine_n%d" % n,
    )(x, rt, y2, y2)


def _dispatch_plan(rt_tiles, hist_tiles, tile_sizes, *, blk):
    hist = jnp.concatenate(hist_tiles, axis=0)[:, :N_EXPERTS].astype(I32)
    counts = jnp.sum(hist, axis=0)
    padded = (counts + blk - 1) // blk * blk
    pend = jnp.cumsum(padded)
    pstart = pend - padded
    tile_base = pstart[None, :] + jnp.cumsum(hist, axis=0) - hist
    ids = jnp.arange(N_EXPERTS, dtype=I32)
    dests, t0 = [], 0
    for rt, ht, ts in zip(rt_tiles, hist_tiles, tile_sizes):
        nt = ht.shape[0]
        tb = jnp.repeat(tile_base[t0:t0 + nt], ts, axis=0)
        t0 += nt
        e = rt[:, RT_E0:RT_E1 + 1].astype(I32)
        r = rt[:, RT_R0:RT_R1 + 1].astype(I32)
        base = jnp.sum(jnp.where(e[:, :, None] == ids[None, None, :], tb[:, None, :], 0), axis=-1)
        dests.append((base + r).reshape(-1))
    return dests, pend


def kernel(x_prompt, x_sample, cache_win_k, cache_win_v, state_pool, cache_mem_k, cache_mem_v,
           mem_prompt, rel_bias, norm_mix_g, w_in, q_norm_g, k_norm_g, pool_w, pool_scale, w_out,
           norm_cross_g, norm_mem_g, w_cq, w_ck, w_cv, cq_norm_g, ck_norm_g, w_co,
           norm_ffn_g, w_rg, b_rg, w_re, b_re, w_e_gate, w_e_up, w_e_down):
    assert norm_mix_g.shape[0] == 1, "single-layer step"
    bp, seq, d = x_prompt.shape
    bs = x_sample.shape[0]
    wb = cache_win_k.shape[2]
    np_tok = bp * seq

    w_in_l = w_in[0]
    w_in_bf = jnp.concatenate([w_in_l[:, 3 * A_W:], w_in_l[:, :3 * A_W]], axis=1).astype(BF16)
    cg_in = jnp.concatenate([jnp.ones((POOL_W,), F32), jnp.tile(q_norm_g[0], A_HEADS) * A_SCALE,
                             jnp.tile(k_norm_g[0], A_HEADS), jnp.ones((A_W,), F32)])[None, :]
    g_mix = norm_mix_g[0][None, :]
    ng = len(POOL_WINDOWS)
    pw_bd = jnp.concatenate(
        [jnp.pad(pool_w[0, g], ((0, 0), (g * POOL_C, (ng - 1 - g) * POOL_C))) for g in range(ng)],
        axis=0).astype(BF16)
    ps = pool_scale[0][None, :]
    woa = w_out[0, :A_W].astype(BF16)
    wop = w_out[0, A_W:].astype(BF16)
    w_mem_bf = jnp.concatenate([w_ck[0], w_cv[0]], axis=1).astype(BF16)
    cg_mem = jnp.concatenate([jnp.tile(ck_norm_g[0], X_HEADS), jnp.ones((X_W,), F32)])[None, :]
    wq_bf = w_cq[0].astype(BF16)
    wo_bf = w_co[0].astype(BF16)
    wr_bf = jnp.pad(jnp.concatenate([w_rg[0], w_re[0]], axis=1),
                    ((0, 0), (0, LANES - N_GROUPS - N_EXPERTS))).astype(BF16)
    br = jnp.pad(jnp.concatenate([b_rg[0], b_re[0]]), (0, LANES - N_GROUPS - N_EXPERTS))[None, :]
    bias_tab = _prompt_bias_table(rel_bias)
    bias_s, bias0 = _step_bias_table(rel_bias, wb)

    proj_in = functools.partial(_proj, g=g_mix, w_bf=w_in_bf, colgain=cg_in,
                                norm_lo=Q_OFF // MXU_N, norm_hi=V_OFF // MXU_N, group=A_HD)

    zp = proj_in(x_prompt.reshape(np_tok, d), tm=1024, name="proj_in_prompt")
    z3 = zp.reshape(bp, seq, IN_W)
    attn_p = _attn_prompt(z3, bias_tab)
    xp1 = _mix_prompt(z3, attn_p, x_prompt, pw_bd, ps, woa, wop, tm=256)

    xs2 = x_sample.reshape(bs, d)
    zs = proj_in(xs2, tm=bs, name="proj_in_step")
    col = lambda off: zs[:, off:off + A_W].reshape(bs, A_HEADS, A_HD, 1)
    us = zs[:, :POOL_W]
    ck_t = jnp.transpose(cache_win_k[0], (0, 2, 3, 1))
    cv_t = jnp.transpose(cache_win_v[0], (0, 2, 3, 1))
    attn_s, nk_t, nv_t = _step_cache(col(Q_OFF), col(K_OFF), col(V_OFF), ck_t, cv_t, bias_s, bias0, hb=4)
    sp_t = jnp.transpose(state_pool[0], (1, 0, 2))
    xs1 = _mix_step(sp_t, us, attn_s.reshape(bs, A_W).astype(BF16), xs2, pw_bd, ps, woa, wop)

    zm = _proj(mem_prompt.reshape(bp * N_MEM, d), norm_mem_g[0][None, :], w_mem_bf, cg_mem,
               tm=512, norm_lo=0, norm_hi=X_W // MXU_N, group=X_HD, name="mem_kv")
    mk_p = zm[:, :X_W].reshape(bp, N_MEM, X_W)
    mv_p = zm[:, X_W:].reshape(bp, N_MEM, X_W)

    cross = functools.partial(_cross, g=norm_cross_g[0][None, :], wq=wq_bf, cqg=cq_norm_g[0][None, :],
                              wo=wo_bf, gf=norm_ffn_g[0][None, :], wr=wr_bf, brt=br)
    tm_x = 512
    xp2, hp, rtp, histp = cross(xp1, mk_p, mv_p, tm=tm_x)
    xs2b, hs, rts, hists = cross(xs1.reshape(bs, 1, d), cache_mem_k[0].reshape(bs, N_MEM, X_W),
                                 cache_mem_v[0].reshape(bs, N_MEM, X_W), tm=1)

    blk = 256
    n_all = np_tok + bs
    n_blocks = -(-(n_all * TOP_K + N_EXPERTS * (blk - 1)) // blk) | 1
    rtp2, rts2 = rtp.reshape(np_tok, LANES), rts.reshape(bs, LANES)
    dests, pend = _dispatch_plan(
        [rtp2, rts2], [histp.reshape(-1, LANES), hists.reshape(-1, LANES)], [tm_x, 1], blk=blk)
    starts = jnp.arange(n_blocks, dtype=I32) * blk
    blk_e = jnp.minimum(jnp.sum((pend[None, :] <= starts[:, None]).astype(I32), axis=1), N_EXPERTS - 1)
    n_used = (pend[-1:] // blk).astype(I32)
    src = _invmap(jnp.concatenate(dests), pend, n_tok=n_all, n_blocks=n_blocks, blk=blk)
    h_all = jnp.concatenate([hp.reshape(np_tok, HALF), hs.reshape(bs, HALF), jnp.zeros((blk, HALF), U32)])
    y2 = _experts(blk_e, n_used, src, h_all, w_e_gate[0], w_e_up[0], w_e_down[0], blk=blk)
    y_prompt = _combine(xp2.reshape(np_tok, d), rtp2, y2, tm=256, row0=0).reshape(bp, seq, d)
    y_sample = _combine(xs2b.reshape(bs, d), rts2, y2, tm=bs, row0=np_tok).reshape(bs, 1, d)

    win_k_p = z3[:, seq - min(MAX_DISTANCE, seq):, K_OFF:V_OFF].reshape(1, bp, -1, A_HEADS, A_HD)
    win_v_p = z3[:, seq - min(MAX_DISTANCE, seq):, V_OFF:].reshape(1, bp, -1, A_HEADS, A_HD)
    win_k_s = jnp.transpose(nk_t, (0, 3, 1, 2))[None]
    win_v_s = jnp.transpose(nv_t, (0, 3, 1, 2))[None]
    pool_p = z3[:, seq - POOL_HIST:, :POOL_W][None]
    pool_s = jnp.transpose(jnp.concatenate([sp_t[1:], us[None]], axis=0), (1, 0, 2))[None]
    mem_k_p = mk_p.reshape(1, bp, N_MEM, X_HEADS, X_HD)
    mem_v_p = mv_p.reshape(1, bp, N_MEM, X_HEADS, X_HD)
    return (y_prompt, y_sample, win_k_p, win_v_p, win_k_s, win_v_s, pool_p, pool_s, mem_k_p, mem_v_p)
```

```python
import functools

import numpy as np
import jax
import jax.numpy as jnp
from jax import lax
from jax.experimental import pallas as pl
from jax.experimental.pallas import tpu as pltpu

F32 = jnp.float32
BF16 = jnp.bfloat16
U32 = jnp.uint32
I32 = jnp.int32

D_MODEL = 2048
A_HEADS = 12
A_HD = 64
A_W = A_HEADS * A_HD
BRANCHES = ((128, 1), (512, 4), (2048, 16))
NK = 128
NUM_BUCKETS = 32
MAX_DISTANCE = 2048
POOL_WINDOWS = (2, 4, 8, 16)
POOL_W = D_MODEL - A_W
POOL_C = POOL_W // len(POOL_WINDOWS)
POOL_HIST = 15
IN_W = 3 * A_W + POOL_W
N_MEM = 256
X_HEADS = 4
X_HD = 128
X_W = X_HEADS * X_HD
X_SCALE = X_HD ** -0.5
A_SCALE = A_HD ** -0.5
N_GROUPS = 4
E_PER_GROUP = 8
N_EXPERTS = N_GROUPS * E_PER_GROUP
TOP_K = 2
D_EXPERT = 512
EPS = 1e-6
NEG = -1e30

LANES = 128
MXU_N = 256
VMEM_LIMIT = 52 * 1024 * 1024
HALF = D_MODEL // 2

U_OFF, Q_OFF, K_OFF, V_OFF = 0, POOL_W, POOL_W + A_W, POOL_W + 2 * A_W

RT_E0, RT_E1, RT_G0, RT_G1, RT_R0, RT_R1 = range(6)


def _cparams(sem):
    return pltpu.CompilerParams(dimension_semantics=sem, vmem_limit_bytes=VMEM_LIMIT)


def _rms_rows(x, g):
    ms = jnp.mean(x * x, axis=-1, keepdims=True)
    return x * lax.rsqrt(ms + EPS) * g


def _pack_pairs(x):
    bits = lax.bitcast_convert_type(x.astype(BF16).astype(F32), U32)
    half = x.shape[1] // 2
    return bits[:, :half] | (bits[:, half:] >> 16)


def _unpack_pairs(u):
    return (lax.bitcast_convert_type(u & jnp.uint32(0xFFFF0000), F32), lax.bitcast_convert_type(u << 16, F32))


def _t5_bucket(dist):
    n = np.asarray(dist, dtype=np.int64)
    exact = NUM_BUCKETS // 2
    ratio = np.log(np.maximum(n, 1) / exact) / np.log(MAX_DISTANCE / exact)
    large = np.minimum(exact + (ratio * (NUM_BUCKETS - exact)).astype(np.int64), NUM_BUCKETS - 1)
    return np.where(n < exact, n, large).astype(np.int32)


def _bias_lookup(rel_bias, buckets, ok):
    buckets = np.asarray(buckets, np.int32)
    onehot = (jnp.arange(NUM_BUCKETS, dtype=I32)[:, None] == jnp.asarray(buckets.reshape(1, -1))).astype(F32)
    vals = jnp.einsum("bh,bn->hn", rel_bias.astype(F32), onehot, precision=lax.Precision.HIGHEST)
    vals = jnp.where(jnp.asarray(np.asarray(ok).reshape(1, -1)), vals, NEG)
    return vals.reshape((A_HEADS,) + buckets.shape)


def _proj_kernel(x_ref, g_ref, w_ref, cg_ref, o_ref, h_scr, *, norm_lo, norm_hi, group):
    j = pl.program_id(1)

    @pl.when(j == 0)
    def _():
        h_scr[...] = _rms_rows(x_ref[...], g_ref[...]).astype(BF16)

    z = jnp.dot(h_scr[...], w_ref[...], preferred_element_type=F32)
    is_norm = jnp.logical_and(j >= norm_lo, j < norm_hi)

    @pl.when(is_norm)
    def _():
        lo_mask = lax.broadcasted_iota(I32, (1, LANES), 1) < group
        outs = []
        for c in range(z.shape[1] // LANES):
            zc = z[:, c * LANES:(c + 1) * LANES]
            zz = zc * zc
            if group == LANES:
                r = lax.rsqrt(jnp.sum(zz, -1, keepdims=True) * (1.0 / group) + EPS)
            else:
                lo = jnp.sum(jnp.where(lo_mask, zz, 0.0), -1, keepdims=True) * (1.0 / group)
                hi = jnp.sum(jnp.where(lo_mask, 0.0, zz), -1, keepdims=True) * (1.0 / group)
                r = jnp.where(lo_mask, lax.rsqrt(lo + EPS), lax.rsqrt(hi + EPS))
            outs.append(zc * r)
        o_ref[...] = jnp.concatenate(outs, -1) * cg_ref[...]

    @pl.when(jnp.logical_not(is_norm))
    def _():
        o_ref[...] = z


def _proj(x, g, w_bf, colgain, *, tm, norm_lo, norm_hi, group, name):
    n, d = x.shape
    width = w_bf.shape[1]
    kern = functools.partial(_proj_kernel, norm_lo=norm_lo, norm_hi=norm_hi, group=group)
    return pl.pallas_call(
        kern,
        grid=(n // tm, width // MXU_N),
        in_specs=[
            pl.BlockSpec((tm, d), lambda i, j: (i, 0)),
            pl.BlockSpec((1, d), lambda i, j: (0, 0)),
            pl.BlockSpec((d, MXU_N), lambda i, j: (0, j)),
            pl.BlockSpec((1, MXU_N), lambda i, j: (0, j)),
        ],
        out_specs=pl.BlockSpec((tm, MXU_N), lambda i, j: (i, j)),
        out_shape=jax.ShapeDtypeStruct((n, width), F32),
        scratch_shapes=[pltpu.VMEM((tm, d), BF16)],
        compiler_params=_cparams(("parallel", "arbitrary")),
        name=name,
    )(x, g, w_bf, colgain)


def _attn_prompt_kernel(q_ref, k_ref, v_ref, bias_ref, o_ref, m_scr, l_scr, acc_scr, *, seq):
    lo = lax.broadcasted_iota(I32, (NK, LANES), 1) < A_HD

    for br, (_, d) in enumerate(BRANCHES):
        nblk = seq // (d * NK)

        def rows(start, d=d):
            return pl.ds(start, NK, stride=d) if d > 1 else pl.ds(start, NK)

        for r in range(d):
            for qb in range(nblk):
                cur = rows(r + d * NK * qb)
                dst = pl.ds((r * nblk + qb) * NK, NK)
                qf = q_ref[cur, :]
                if qb > 0:
                    prev = rows(r + d * NK * (qb - 1))
                    kk = jnp.concatenate([k_ref[prev, :], k_ref[cur, :]], 0)
                    vv = jnp.concatenate([v_ref[prev, :], v_ref[cur, :]], 0)
                    off = 0
                else:
                    kk, vv, off = k_ref[cur, :], v_ref[cur, :], NK
                kk = kk.astype(BF16)
                vv = vv.astype(BF16)
                ms, ls, pvs = [], [], []
                for h in range(2):
                    qh = jnp.where(lo if h == 0 else jnp.logical_not(lo), qf, 0.0).astype(BF16)
                    s = lax.dot_general(qh, kk, (((1,), (1,)), ((), ())), preferred_element_type=F32)
                    s = s + bias_ref[br, h, :, off:2 * NK]
                    m = jnp.max(s, -1, keepdims=True)
                    p = jnp.exp(s - m)
                    ms.append(m)
                    ls.append(jnp.sum(p, -1, keepdims=True))
                    pvs.append(jnp.dot(p.astype(BF16), vv, preferred_element_type=F32))
                acc_scr[br, dst, :] = jnp.where(lo, pvs[0], pvs[1])
                m_scr[br, dst, :] = jnp.where(lo, ms[0], ms[1])
                l_scr[br, dst, :] = jnp.where(lo, ls[0], ls[1])

    d4, d16 = BRANCHES[1][1], BRANCHES[2][1]
    for r in range(d16):
        for qb in range(seq // (d16 * NK)):
            nat = pl.ds(r + d16 * NK * qb, NK, stride=d16)
            views = (nat,
                     pl.ds((r % d4) * (seq // d4) + r // d4 + (d16 // d4) * NK * qb, NK, stride=d16 // d4),
                     pl.ds(r * (seq // d16) + NK * qb, NK))
            m = [m_scr[i, v, :] for i, v in enumerate(views)]
            mx = jnp.maximum(jnp.maximum(m[0], m[1]), m[2])
            w = [jnp.exp(mi - mx) for mi in m]
            num = sum(wi * acc_scr[i, v, :] for i, (wi, v) in enumerate(zip(w, views)))
            den = sum(wi * l_scr[i, v, :] for i, (wi, v) in enumerate(zip(w, views)))
            o_ref[nat, :] = num / den


def _attn_prompt(z3, bias_tab):
    b, seq, _ = z3.shape
    assert seq % BRANCHES[-1][0] == 0, "sequence must be a multiple of the widest window"
    nhp = A_W // LANES
    nbr = len(BRANCHES)
    qo, ko, vo = Q_OFF // LANES, K_OFF // LANES, V_OFF // LANES
    kern = functools.partial(_attn_prompt_kernel, seq=seq)
    return pl.pallas_call(
        kern,
        grid=(b, nhp),
        in_specs=[
            pl.BlockSpec((None, seq, LANES), lambda i, hp: (i, 0, qo + hp)),
            pl.BlockSpec((None, seq, LANES), lambda i, hp: (i, 0, ko + hp)),
            pl.BlockSpec((None, seq, LANES), lambda i, hp: (i, 0, vo + hp)),
            pl.BlockSpec((len(BRANCHES), None, 2, NK, 2 * NK), lambda i, hp: (0, hp, 0, 0, 0)),
        ],
        out_specs=pl.BlockSpec((None, seq, LANES), lambda i, hp: (i, 0, hp)),
        out_shape=jax.ShapeDtypeStruct((b, seq, A_W), F32),
        scratch_shapes=[pltpu.VMEM((nbr, seq, LANES), F32)] * 3,
        compiler_params=_cparams(("parallel", "parallel")),
        name="attn_prompt",
    )(z3, z3, z3, bias_tab)


def _prompt_bias_table(rel_bias):
    qi = np.arange(NK)[:, None]
    kj = np.arange(2 * NK)[None, :]
    rel = qi + NK - kj
    ok = (rel >= 0) & (rel <= NK)
    bkt = np.stack([_t5_bucket(np.clip(rel, 0, None) * d) for _, d in BRANCHES])
    tab = _bias_lookup(rel_bias, bkt, np.broadcast_to(ok, bkt.shape))
    return jnp.transpose(tab, (1, 0, 2, 3)).reshape(len(BRANCHES), A_HEADS // 2, 2, NK, 2 * NK)


BR_ROWS = 8


def _step_cache_kernel(q_ref, kn_ref, vn_ref, ck_ref, cv_ref, bias_ref, b0_ref, o_ref, ok_ref, ov_ref,
                       *, hb, wb):
    last = lax.broadcasted_iota(I32, (1, wb), 1) == wb - 1
    rnd = lambda t: t.astype(BF16).astype(F32)
    for h in range(hb):
        q = rnd(q_ref[h])
        kt, vt = ck_ref[h], cv_ref[h]
        kn, vn = kn_ref[h], vn_ref[h]
        s = jnp.sum(rnd(kt) * q, axis=0, keepdims=True)
        sb = s + bias_ref[h]
        s_new = jnp.sum(rnd(kn) * q, axis=0, keepdims=True) + b0_ref[h]
        m = jnp.maximum(jnp.max(sb, axis=1, keepdims=True), s_new)
        p = jnp.exp(sb - m)
        p_new = jnp.exp(s_new - m)
        l = jnp.sum(p, axis=1, keepdims=True) + p_new
        is_branch = lax.broadcasted_iota(I32, (BR_ROWS, 1), 0) < len(BRANCHES)
        lse = jnp.where(is_branch, m + jnp.log(l), NEG)
        e = jnp.exp(lse - jnp.max(lse, axis=0, keepdims=True))
        wts = e / jnp.sum(e, axis=0, keepdims=True)
        pr, pnr, vtr, vnr = rnd(p), rnd(p_new), rnd(vt), rnd(vn)
        acc = jnp.zeros_like(vn)
        for br in range(len(BRANCHES)):
            rows = slice(br, br + 1)
            o_br = (jnp.sum(vtr * pr[rows], axis=1, keepdims=True) + pnr[rows] * vnr) / l[rows]
            acc = acc + wts[rows] * o_br
        o_ref[h] = acc
        ok_ref[h] = jnp.where(last, kn, pltpu.roll(kt, wb - 1, 1))
        ov_ref[h] = jnp.where(last, vn, pltpu.roll(vt, wb - 1, 1))


def _step_cache(qc, knc, vnc, ck_t, cv_t, bias_s, bias0, *, hb):
    b, _, _, wb = ck_t.shape
    col = pl.BlockSpec((None, hb, A_HD, 1), lambda i, j: (i, j, 0, 0))
    cache = pl.BlockSpec((None, hb, A_HD, wb), lambda i, j: (i, j, 0, 0))
    kern = functools.partial(_step_cache_kernel, hb=hb, wb=wb)
    return pl.pallas_call(
        kern,
        grid=(b, A_HEADS // hb),
        in_specs=[col, col, col, cache, cache,
                  pl.BlockSpec((hb, BR_ROWS, wb), lambda i, j: (j, 0, 0)),
                  pl.BlockSpec((hb, 1, 1), lambda i, j: (j, 0, 0))],
        out_specs=[col, cache, cache],
        out_shape=[jax.ShapeDtypeStruct((b, A_HEADS, A_HD, 1), F32),
                   jax.ShapeDtypeStruct(ck_t.shape, F32),
                   jax.ShapeDtypeStruct(cv_t.shape, F32)],
        compiler_params=_cparams(("parallel", "parallel")),
        name="step_cache",
    )(qc, knc, vnc, ck_t, cv_t, bias_s, bias0)


def _step_bias_table(rel_bias, wb):
    dist = wb - np.arange(wb)
    bkt = np.zeros((BR_ROWS, wb), np.int32)
    ok = np.zeros((BR_ROWS, wb), bool)
    for br, (w, d) in enumerate(BRANCHES):
        bkt[br] = _t5_bucket(dist)
        ok[br] = (dist <= w) & (dist % d == 0)
    tab = _bias_lookup(rel_bias, bkt, ok)
    bias0 = _bias_lookup(rel_bias, _t5_bucket(np.zeros((1, 1))), np.ones((1, 1), bool))
    return tab, bias0


def _mix_tail(pooled, attn, x, pw_ref, ps_ref, woa_ref, wop_ref):
    pb = jnp.dot(pooled.astype(BF16), pw_ref[...], preferred_element_type=F32) * ps_ref[...]
    y = jnp.dot(attn.astype(BF16), woa_ref[...], preferred_element_type=F32)
    y = y + jnp.dot(pb.astype(BF16), wop_ref[...], preferred_element_type=F32)
    return x + y


def _pick_by_group(vals):
    col = lax.broadcasted_iota(I32, (1, POOL_W), 1)
    out = vals[-1]
    for g in range(len(vals) - 2, -1, -1):
        out = jnp.where(col < (g + 1) * POOL_C, vals[g], out)
    return out


def _mix_prompt_kernel(u_ref, uh_ref, a_ref, x_ref, pw_ref, ps_ref, woa_ref, wop_ref, o_ref, *, tm):
    i = pl.program_id(1)
    u = u_ref[...]
    halo = jnp.where(i > 0, uh_ref[...], 0.0)
    e = jnp.concatenate([halo, u], 0)
    hr = halo.shape[0]
    sums, s, w = [], e, 1
    while w < POOL_WINDOWS[-1]:
        s = s + pltpu.roll(s, w, 0)
        w *= 2
        sums.append(s)
    pos = i * tm + lax.broadcasted_iota(I32, (tm, 1), 0)
    means = [sw[hr:] / jnp.minimum(wd, pos + 1).astype(F32) for sw, wd in zip(sums, POOL_WINDOWS)]
    pooled = _pick_by_group(means) - u
    o_ref[...] = _mix_tail(pooled, a_ref[...], x_ref[...], pw_ref, ps_ref, woa_ref, wop_ref)


def _mix_weight_specs():
    zero = lambda *_: (0, 0)
    return [
        pl.BlockSpec((POOL_W, POOL_W), zero),
        pl.BlockSpec((1, POOL_W), zero),
        pl.BlockSpec((A_W, D_MODEL), zero),
        pl.BlockSpec((POOL_W, D_MODEL), zero),
    ]


def _mix_prompt(z3, attn, x3, pw_bd, ps, woa, wop, *, tm):
    b, seq, _ = x3.shape
    hr = 16
    kern = functools.partial(_mix_prompt_kernel, tm=tm)
    return pl.pallas_call(
        kern,
        grid=(b, seq // tm),
        in_specs=[
            pl.BlockSpec((None, tm, POOL_W), lambda bi, i: (bi, i, 0)),
            pl.BlockSpec((None, hr, POOL_W), lambda bi, i: (bi, jnp.maximum(i * (tm // hr) - 1, 0), 0)),
            pl.BlockSpec((None, tm, A_W), lambda bi, i: (bi, i, 0)),
            pl.BlockSpec((None, tm, D_MODEL), lambda bi, i: (bi, i, 0)),
        ] + _mix_weight_specs(),
        out_specs=pl.BlockSpec((None, tm, D_MODEL), lambda bi, i: (bi, i, 0)),
        out_shape=jax.ShapeDtypeStruct((b, seq, D_MODEL), F32),
        compiler_params=_cparams(("parallel", "parallel")),
        name="mix_prompt",
    )(z3, z3, attn, x3, pw_bd, ps, woa, wop)


def _mix_step_kernel(sp_ref, u_ref, a_ref, x_ref, pw_ref, ps_ref, woa_ref, wop_ref, o_ref):
    u = u_ref[...]
    means, hist, j = [], u, POOL_HIST
    for w in POOL_WINDOWS:
        while j > POOL_HIST - (w - 1):
            j -= 1
            hist = hist + sp_ref[j]
        means.append(hist / float(w))
    pooled = _pick_by_group(means) - u
    o_ref[...] = _mix_tail(pooled, a_ref[...], x_ref[...], pw_ref, ps_ref, woa_ref, wop_ref)


def _mix_step(sp_t, us, attn, xs, pw_bd, ps, woa, wop):
    b = xs.shape[0]
    full = lambda shape: pl.BlockSpec(shape, lambda i: (0,) * len(shape))
    return pl.pallas_call(
        _mix_step_kernel,
        grid=(1,),
        in_specs=[full(sp_t.shape), full(us.shape), full(attn.shape), full(xs.shape)] + _mix_weight_specs(),
        out_specs=full((b, D_MODEL)),
        out_shape=jax.ShapeDtypeStruct((b, D_MODEL), F32),
        compiler_params=_cparams(("arbitrary",)),
        name="mix_step",
    )(sp_t, us, attn, xs, pw_bd, ps, woa, wop)


def _first_argmax(vals, lane):
    mx = jnp.max(vals, -1, keepdims=True)
    idx = jnp.min(jnp.where(vals == mx, lane, LANES), -1, keepdims=True)
    return mx, idx


def _cross_kernel(x_ref, g_ref, wq_ref, cqg_ref, mk_ref, mv_ref, wo_ref, gf_ref, wr_ref, br_ref,
                  y_ref, h_ref, rt_ref, hist_ref):
    x = x_ref[...]
    tm = x.shape[0]
    hq = _rms_rows(x, g_ref[...]).astype(BF16)
    q = jnp.dot(hq, wq_ref[...], preferred_element_type=F32)
    outs = []
    for h in range(X_HEADS):
        sl = slice(h * X_HD, (h + 1) * X_HD)
        qh = _rms_rows(q[:, sl], cqg_ref[...]).astype(BF16)
        kh = mk_ref[:, sl].astype(BF16)
        vh = mv_ref[:, sl].astype(BF16)
        s = lax.dot_general(qh, kh, (((1,), (1,)), ((), ())), preferred_element_type=F32) * X_SCALE
        p = jnp.exp(s - jnp.max(s, -1, keepdims=True))
        p = p / jnp.sum(p, -1, keepdims=True)
        outs.append(jnp.dot(p.astype(BF16), vh, preferred_element_type=F32))
    o = jnp.concatenate(outs, -1).astype(BF16)
    y = x + jnp.dot(o, wo_ref[...], preferred_element_type=F32)
    y_ref[...] = y

    hf = _rms_rows(y, gf_ref[...])
    h_bf = hf.astype(BF16)
    h_ref[...] = _pack_pairs(hf)

    lg = jnp.dot(h_bf, wr_ref[...], preferred_element_type=F32) + br_ref[...]

    lane = lax.broadcasted_iota(I32, lg.shape, 1)
    lgg = jnp.where(lane < N_GROUPS, lg, NEG)
    mg, g_sel = _first_argmax(lgg, lane)
    p_grp = 1.0 / jnp.sum(jnp.exp(lgg - mg), -1, keepdims=True)
    e_lo = N_GROUPS + E_PER_GROUP * g_sel
    le = jnp.where(jnp.logical_and(lane >= e_lo, lane < e_lo + E_PER_GROUP), lg, NEG)
    v0, i0 = _first_argmax(le, lane)
    v1, i1 = _first_argmax(jnp.where(lane == i0, NEG, le), lane)
    e1w = jnp.exp(v1 - v0)
    g0 = p_grp / (1.0 + e1w)
    g1 = p_grp * e1w / (1.0 + e1w)
    e0 = i0 - N_GROUPS
    e1 = i1 - N_GROUPS

    cnt = jnp.logical_or(lane == e0, lane == e1).astype(F32)
    if tm > 1:
        ri = lax.broadcasted_iota(I32, (tm, tm), 0)
        ci = lax.broadcasted_iota(I32, (tm, tm), 1)
        before = (ci < ri).astype(BF16)
        pre = jnp.dot(before, cnt.astype(BF16), preferred_element_type=F32)
        r0 = jnp.sum(jnp.where(lane == e0, pre, 0.0), -1, keepdims=True)
        r1 = jnp.sum(jnp.where(lane == e1, pre, 0.0), -1, keepdims=True)
    else:
        r0 = r1 = jnp.zeros_like(g0)
    hist_ref[...] = jnp.sum(cnt, axis=0, keepdims=True)

    rec = jnp.zeros(lg.shape, F32)
    for ln, val in ((RT_E0, e0.astype(F32)), (RT_E1, e1.astype(F32)), (RT_G0, g0), (RT_G1, g1),
                    (RT_R0, r0), (RT_R1, r1)):
        rec = jnp.where(lane == ln, val, rec)
    rt_ref[...] = rec


def _cross(x3, mk, mv, g, wq, cqg, wo, gf, wr, brt, *, tm):
    b, seq, _ = x3.shape
    nt = seq // tm
    zero = lambda bi, i: (0, 0)
    tok = lambda w: pl.BlockSpec((None, tm, w), lambda bi, i: (bi, i, 0))
    mem = pl.BlockSpec((None, N_MEM, X_W), lambda bi, i: (bi, 0, 0))
    return pl.pallas_call(
        _cross_kernel,
        grid=(b, nt),
        in_specs=[
            tok(D_MODEL),
            pl.BlockSpec((1, D_MODEL), zero),
            pl.BlockSpec((D_MODEL, X_W), zero),
            pl.BlockSpec((1, X_HD), zero),
            mem, mem,
            pl.BlockSpec((X_W, D_MODEL), zero),
            pl.BlockSpec((1, D_MODEL), zero),
            pl.BlockSpec((D_MODEL, LANES), zero),
            pl.BlockSpec((1, LANES), zero),
        ],
        out_specs=[tok(D_MODEL), tok(HALF), tok(LANES),
                   pl.BlockSpec((None, None, 1, LANES), lambda bi, i: (bi, i, 0, 0))],
        out_shape=[
            jax.ShapeDtypeStruct((b, seq, D_MODEL), F32),
            jax.ShapeDtypeStruct((b, seq, HALF), U32),
            jax.ShapeDtypeStruct((b, seq, LANES), F32),
            jax.ShapeDtypeStruct((b, nt, 1, LANES), F32),
        ],
        compiler_params=_cparams(("parallel", "parallel")),
        name="cross_b%d" % b,
    )(x3, g, wq, cqg, mk, mv, wo, gf, wr, brt)


def _pad_rows(n_tok, blk):
    return -(-(n_tok + blk) // blk) * blk


def _invmap_kernel(dest_ref, pend_ref, src_ref, *, n_assign, n_tok, n_blocks, blk):
    npad = _pad_rows(n_tok, blk)

    def fill(base):
        first = (lax.div(base, blk) & 1) * npad + n_tok

        def body(r, c):
            src_ref[base + r] = first + r
            return c
        lax.fori_loop(0, blk, body, 0, unroll=8)

    for e in range(N_EXPERTS):
        fill(jnp.maximum(pend_ref[e] - blk, 0))

    def fill_block(j, c):
        fill(j * blk)
        return c

    lax.fori_loop(pend_ref[N_EXPERTS - 1] // blk, n_blocks + 1, fill_block, 0)

    def put(a, c):
        src_ref[dest_ref[a]] = (a & 1) * npad + (a >> 1)
        return c

    lax.fori_loop(0, n_assign, put, 0, unroll=8)


def _invmap(dest, pend, *, n_tok, n_blocks, blk):
    assert n_blocks % 2 == 1, "the spare block must have odd parity"
    smem = pl.BlockSpec(memory_space=pltpu.SMEM)
    kern = functools.partial(_invmap_kernel, n_assign=dest.shape[0], n_tok=n_tok, n_blocks=n_blocks, blk=blk)
    return pl.pallas_call(
        kern,
        in_specs=[smem, smem],
        out_specs=smem,
        out_shape=jax.ShapeDtypeStruct(((n_blocks + 1) * blk,), I32),
        name="moe_invmap",
    )(dest, pend)


def _expert_kernel(be_ref, nb_ref, src_ref, h_ref, wg_ref, wu_ref, wd_ref, y2_ref,
                   xbuf0, xbuf1, obuf0, obuf1, wg_bf, wu_bf, wd_bf, gsem, ssem, *, blk, n_tok):
    i = pl.program_id(0)
    nb = nb_ref[0]
    n_rows = pl.num_programs(0) * blk
    npad = _pad_rows(n_tok, blk)
    xbufs, obufs = (xbuf0, xbuf1), (obuf0, obuf1)

    def gather_copy(tok, r, slot):
        return pltpu.make_async_copy(h_ref.at[pl.ds(tok, 1)], xbufs[slot].at[pl.ds(r, 1)], gsem.at[slot])

    def scatter_copy(r, dst, slot):
        return pltpu.make_async_copy(obufs[slot].at[pl.ds(r, 1)], y2_ref.at[pl.ds(dst, 1)], ssem.at[slot])

    def gather_start(base, slot):
        for r in range(blk):
            row = src_ref[base + r]
            gather_copy(jnp.where(row >= npad, row - npad, row), r, slot).start()

    def gather_wait(slot):
        for r in range(blk):
            gather_copy(0, r, slot).wait()

    def scatter_start(base, slot):
        for r in range(blk):
            scatter_copy(r, src_ref[base + r], slot).start()

    def scatter_wait(slot):
        for r in range(blk):
            scatter_copy(r, 0, slot).wait()

    @pl.when(i == 0)
    def _():
        obuf1[...] = jnp.zeros(obuf1.shape, U32)
        tail = [half * npad + n_tok + r for half in range(TOP_K) for r in range(npad - n_tok)]
        for t, dst in enumerate(tail):
            scatter_copy(t % blk, dst, 1).start()
        for t in range(len(tail)):
            scatter_copy(t % blk, 0, 1).wait()
        gather_start(0, 0)

    def step(slot):
        other = 1 - slot
        gather_wait(slot)
        changed = jnp.logical_or(i == 0, be_ref[i] != be_ref[jnp.maximum(i - 1, 0)])

        @pl.when(changed)
        def _():
            wg_bf[...] = wg_ref[...].astype(BF16)
            wu_bf[...] = wu_ref[...].astype(BF16)
            wd_bf[...] = wd_ref[...].astype(BF16)

        gather_start(jnp.minimum(i + 1, nb - 1) * blk, other)
        scatter_start(jnp.where(i > 0, (i - 1) * blk, n_rows), other)
        x = jnp.concatenate([t.astype(BF16) for t in _unpack_pairs(xbufs[slot][...])], -1)
        a = jax.nn.silu(jnp.dot(x, wg_bf[...], preferred_element_type=F32))
        b = jnp.dot(x, wu_bf[...], preferred_element_type=F32)
        obufs[slot][...] = _pack_pairs(jnp.dot((a * b).astype(BF16), wd_bf[...], preferred_element_type=F32))
        scatter_wait(other)

        @pl.when(i == nb - 1)
        def _():
            scatter_start(i * blk, slot)
            scatter_wait(slot)
            gather_wait(other)

    @pl.when(jnp.logical_and(i < nb, i % 2 == 0))
    def _():
        step(0)

    @pl.when(jnp.logical_and(i < nb, i % 2 == 1))
    def _():
        step(1)


def _experts(blk_e, n_used, src, h_all, w_gate, w_up, w_down, *, blk):
    n_tok = h_all.shape[0] - blk
    wspec = lambda k, n: pl.BlockSpec((None, k, n), lambda i, be, nb, src: (be[i], 0, 0))
    return pl.pallas_call(
        functools.partial(_expert_kernel, blk=blk, n_tok=n_tok),
        grid_spec=pltpu.PrefetchScalarGridSpec(
            num_scalar_prefetch=3,
            grid=(blk_e.shape[0],),
            in_specs=[pl.BlockSpec(memory_space=pl.ANY),
                      wspec(D_MODEL, D_EXPERT), wspec(D_MODEL, D_EXPERT), wspec(D_EXPERT, D_MODEL)],
            out_specs=pl.BlockSpec(memory_space=pl.ANY),
            scratch_shapes=[
                pltpu.VMEM((blk, HALF), U32), pltpu.VMEM((blk, HALF), U32),
                pltpu.VMEM((blk, HALF), U32), pltpu.VMEM((blk, HALF), U32),
                pltpu.VMEM((D_MODEL, D_EXPERT), BF16),
                pltpu.VMEM((D_MODEL, D_EXPERT), BF16),
                pltpu.VMEM((D_EXPERT, D_MODEL), BF16),
                pltpu.SemaphoreType.DMA((2,)), pltpu.SemaphoreType.DMA((2,)),
            ],
        ),
        out_shape=jax.ShapeDtypeStruct((TOP_K * _pad_rows(n_tok, blk), HALF), U32),
        compiler_params=_cparams(("arbitrary",)),
        name="experts",
    )(blk_e, n_used, src, h_all, w_gate, w_up, w_down)


def _combine_kernel(x_ref, rt_ref, y0_ref, y1_ref, o_ref):
    rt = rt_ref[...]
    g0 = rt[:, RT_G0:RT_G0 + 1]
    g1 = rt[:, RT_G1:RT_G1 + 1]
    y0, y1 = _unpack_pairs(y0_ref[...]), _unpack_pairs(y1_ref[...])
    o_ref[:, :HALF] = x_ref[:, :HALF] + (g0 * y0[0] + g1 * y1[0])
    o_ref[:, HALF:] = x_ref[:, HALF:] + (g0 * y0[1] + g1 * y1[1])


def _combine(x, rt, y2, *, tm, row0):
    n = x.shape[0]
    npad = y2.shape[0] // TOP_K
    off, half = row0 // tm, npad // tm
    assert off * tm == row0 and half * tm == npad
    return pl.pallas_call(
        _combine_kernel,
        grid=(n // tm,),
        in_specs=[pl.BlockSpec((tm, D_MODEL), lambda i: (i, 0)),
                  pl.BlockSpec((tm, LANES), lambda i: (i, 0)),
                  pl.BlockSpec((tm, HALF), lambda i: (i + off, 0)),
                  pl.BlockSpec((tm, HALF), lambda i: (i + off + half, 0))],
        out_specs=pl.BlockSpec((tm, D_MODEL), lambda i: (i, 0)),
        out_shape=jax.ShapeDtypeStruct((n, D_MODEL), F32),
        compiler_params=_cparams(("parallel",)),
        name="combine_n%d" % n,
    )(x, rt, y2, y2)


def _dispatch_plan(rt_tiles, hist_tiles, tile_sizes, *, blk):
    hist = jnp.concatenate(hist_tiles, axis=0)[:, :N_EXPERTS].astype(I32)
    counts = jnp.sum(hist, axis=0)
    padded = (counts + blk - 1) // blk * blk
    pend = jnp.cumsum(padded)
    pstart = pend - padded
    tile_base = pstart[None, :] + jnp.cumsum(hist, axis=0) - hist
    ids = jnp.arange(N_EXPERTS, dtype=I32)
    dests, t0 = [], 0
    for rt, ht, ts in zip(rt_tiles, hist_tiles, tile_sizes):
        nt = ht.shape[0]
        tb = jnp.repeat(tile_base[t0:t0 + nt], ts, axis=0)
        t0 += nt
        e = rt[:, RT_E0:RT_E1 + 1].astype(I32)
        r = rt[:, RT_R0:RT_R1 + 1].astype(I32)
        base = jnp.sum(jnp.where(e[:, :, None] == ids[None, None, :], tb[:, None, :], 0), axis=-1)
        dests.append((base + r).reshape(-1))
    return dests, pend


def kernel(x_prompt, x_sample, cache_win_k, cache_win_v, state_pool, cache_mem_k, cache_mem_v,
           mem_prompt, rel_bias, norm_mix_g, w_in, q_norm_g, k_norm_g, pool_w, pool_scale, w_out,
           norm_cross_g, norm_mem_g, w_cq, w_ck, w_cv, cq_norm_g, ck_norm_g, w_co,
           norm_ffn_g, w_rg, b_rg, w_re, b_re, w_e_gate, w_e_up, w_e_down):
    assert norm_mix_g.shape[0] == 1, "single-layer step"
    bp, seq, d = x_prompt.shape
    bs = x_sample.shape[0]
    wb = cache_win_k.shape[2]
    np_tok = bp * seq

    w_in_l = w_in[0]
    w_in_bf = jnp.concatenate([w_in_l[:, 3 * A_W:], w_in_l[:, :3 * A_W]], axis=1).astype(BF16)
    cg_in = jnp.concatenate([jnp.ones((POOL_W,), F32), jnp.tile(q_norm_g[0], A_HEADS) * A_SCALE,
                             jnp.tile(k_norm_g[0], A_HEADS), jnp.ones((A_W,), F32)])[None, :]
    g_mix = norm_mix_g[0][None, :]
    ng = len(POOL_WINDOWS)
    pw_bd = jnp.concatenate(
        [jnp.pad(pool_w[0, g], ((0, 0), (g * POOL_C, (ng - 1 - g) * POOL_C))) for g in range(ng)],
        axis=0).astype(BF16)
    ps = pool_scale[0][None, :]
    woa = w_out[0, :A_W].astype(BF16)
    wop = w_out[0, A_W:].astype(BF16)
    w_mem_bf = jnp.concatenate([w_ck[0], w_cv[0]], axis=1).astype(BF16)
    cg_mem = jnp.concatenate([jnp.tile(ck_norm_g[0], X_HEADS), jnp.ones((X_W,), F32)])[None, :]
    wq_bf = w_cq[0].astype(BF16)
    wo_bf = w_co[0].astype(BF16)
    wr_bf = jnp.pad(jnp.concatenate([w_rg[0], w_re[0]], axis=1),
                    ((0, 0), (0, LANES - N_GROUPS - N_EXPERTS))).astype(BF16)
    br = jnp.pad(jnp.concatenate([b_rg[0], b_re[0]]), (0, LANES - N_GROUPS - N_EXPERTS))[None, :]
    bias_tab = _prompt_bias_table(rel_bias)
    bias_s, bias0 = _step_bias_table(rel_bias, wb)

    proj_in = functools.partial(_proj, g=g_mix, w_bf=w_in_bf, colgain=cg_in,
                                norm_lo=Q_OFF // MXU_N, norm_hi=V_OFF // MXU_N, group=A_HD)

    zp = proj_in(x_prompt.reshape(np_tok, d), tm=1024, name="proj_in_prompt")
    z3 = zp.reshape(bp, seq, IN_W)
    attn_p = _attn_prompt(z3, bias_tab)
    xp1 = _mix_prompt(z3, attn_p, x_prompt, pw_bd, ps, woa, wop, tm=256)

    xs2 = x_sample.reshape(bs, d)
    zs = proj_in(xs2, tm=bs, name="proj_in_step")
    col = lambda off: zs[:, off:off + A_W].reshape(bs, A_HEADS, A_HD, 1)
    us = zs[:, :POOL_W]
    ck_t = jnp.transpose(cache_win_k[0], (0, 2, 3, 1))
    cv_t = jnp.transpose(cache_win_v[0], (0, 2, 3, 1))
    attn_s, nk_t, nv_t = _step_cache(col(Q_OFF), col(K_OFF), col(V_OFF), ck_t, cv_t, bias_s, bias0, hb=6)
    sp_t = jnp.transpose(state_pool[0], (1, 0, 2))
    xs1 = _mix_step(sp_t, us, attn_s.reshape(bs, A_W).astype(BF16), xs2, pw_bd, ps, woa, wop)

    zm = _proj(mem_prompt.reshape(bp * N_MEM, d), norm_mem_g[0][None, :], w_mem_bf, cg_mem,
               tm=512, norm_lo=0, norm_hi=X_W // MXU_N, group=X_HD, name="mem_kv")
    mk_p = zm[:, :X_W].reshape(bp, N_MEM, X_W)
    mv_p = zm[:, X_W:].reshape(bp, N_MEM, X_W)

    cross = functools.partial(_cross, g=norm_cross_g[0][None, :], wq=wq_bf, cqg=cq_norm_g[0][None, :],
                              wo=wo_bf, gf=norm_ffn_g[0][None, :], wr=wr_bf, brt=br)
    tm_x = 512
    xp2, hp, rtp, histp = cross(xp1, mk_p, mv_p, tm=tm_x)
    xs2b, hs, rts, hists = cross(xs1.reshape(bs, 1, d), cache_mem_k[0].reshape(bs, N_MEM, X_W),
                                 cache_mem_v[0].reshape(bs, N_MEM, X_W), tm=1)

    blk = 256
    n_all = np_tok + bs
    n_blocks = -(-(n_all * TOP_K + N_EXPERTS * (blk - 1)) // blk) | 1
    rtp2, rts2 = rtp.reshape(np_tok, LANES), rts.reshape(bs, LANES)
    dests, pend = _dispatch_plan(
        [rtp2, rts2], [histp.reshape(-1, LANES), hists.reshape(-1, LANES)], [tm_x, 1], blk=blk)
    starts = jnp.arange(n_blocks, dtype=I32) * blk
    blk_e = jnp.minimum(jnp.sum((pend[None, :] <= starts[:, None]).astype(I32), axis=1), N_EXPERTS - 1)
    n_used = (pend[-1:] // blk).astype(I32)
    src = _invmap(jnp.concatenate(dests), pend, n_tok=n_all, n_blocks=n_blocks, blk=blk)
    h_all = jnp.concatenate([hp.reshape(np_tok, HALF), hs.reshape(bs, HALF), jnp.zeros((blk, HALF), U32)])
    y2 = _experts(blk_e, n_used, src, h_all, w_e_gate[0], w_e_up[0], w_e_down[0], blk=blk)
    y_prompt = _combine(xp2.reshape(np_tok, d), rtp2, y2, tm=256, row0=0).reshape(bp, seq, d)
    y_sample = _combine(xs2b.reshape(bs, d), rts2, y2, tm=bs, row0=np_tok).reshape(bs, 1, d)

    win_k_p = z3[:, seq - min(MAX_DISTANCE, seq):, K_OFF:V_OFF].reshape(1, bp, -1, A_HEADS, A_HD)
    win_v_p = z3[:, seq - min(MAX_DISTANCE, seq):, V_OFF:].reshape(1, bp, -1, A_HEADS, A_HD)
    win_k_s = jnp.transpose(nk_t, (0, 3, 1, 2))[None]
    win_v_s = jnp.transpose(nv_t, (0, 3, 1, 2))[None]
    pool_p = z3[:, seq - POOL_HIST:, :POOL_W][None]
    pool_s = jnp.transpose(jnp.concatenate([sp_t[1:], us[None]], axis=0), (1, 0, 2))[None]
    mem_k_p = mk_p.reshape(1, bp, N_MEM, X_HEADS, X_HD)
    mem_v_p = mv_p.reshape(1, bp, N_MEM, X_HEADS, X_HD)
    return (y_prompt, y_sample, win_k_p, win_v_p, win_k_s, win_v_s, pool_p, pool_s, mem_k_p, mem_v_p)
```
